```python
import jax, jax.numpy as jnp
from jax import lax
import numpy as np

D_MODEL = 1024
BATCH = 4
SEQ = 4096
DEPTH = 2

W_BRANCH = D_MODEL // 2
N_BRANCH = 4
CONV_WIDTH = 4
LRU_BLOCKS = 8
LRU_C = 8.0
MLSTM_HEADS = 4
MLSTM_QKV_BLOCK = 4
HGRN_HEADS = 4
GLA_HEADS = 4
GLA_DK = W_BRANCH // 2
GLA_DV = W_BRANCH
GLA_GATE_RANK = 16
GLA_GATE_TAU = 16.0
CHUNK = 64
NORM_EPS = 1e-6

SPLIT_SIZES = (
    W_BRANCH, W_BRANCH,
    W_BRANCH, W_BRANCH, W_BRANCH,
    W_BRANCH, W_BRANCH, W_BRANCH, W_BRANCH,
    GLA_DK, GLA_DK, GLA_DV, GLA_GATE_RANK, GLA_DV,
    N_BRANCH * D_MODEL,
)
D_IN = sum(SPLIT_SIZES)

kernel_name = "hybrid_lru_mlstm_hgrn2_gla_gated_merge"


def rmsnorm(x, g):
    xf = x.astype(jnp.float32)
    return xf * lax.rsqrt(jnp.mean(xf * xf, axis=-1, keepdims=True) + NORM_EPS) * g


def head_rmsnorm(x, w):
    B, S, H, d = x.shape
    y = x * lax.rsqrt(jnp.mean(x * x, axis=-1, keepdims=True) + NORM_EPS) * w
    return y.reshape(B, S, H * d)


def head_layernorm(x, w, n_heads):
    B, S, W = x.shape
    xh = x.reshape(B, S, n_heads, W // n_heads)
    mu = jnp.mean(xh, axis=-1, keepdims=True)
    xc = xh - mu
    var = jnp.mean(xc * xc, axis=-1, keepdims=True)
    return (xc * lax.rsqrt(var + NORM_EPS)).reshape(B, S, W) * w


def causal_conv(x, w, b):
    K = w.shape[0]
    S = x.shape[1]
    xp = jnp.pad(x, ((0, 0), (K - 1, 0), (0, 0)))
    y = b
    for j in range(K):
        y = y + xp[:, j:j + S] * w[j]
    return y


def headwise(x, w):
    B, S, W = x.shape
    nb, blk, _ = w.shape
    return jnp.einsum('bsni,nij->bsnj', x.reshape(B, S, nb, blk), w).reshape(B, S, W)


def split_heads(x, n_heads):
    B, S, W = x.shape
    return x.reshape(B, S, n_heads, W // n_heads)


def _to_chunks(t):
    B, S, H = t.shape[:3]
    t = t.reshape((B, S // CHUNK, CHUNK, H) + t.shape[3:])
    return jnp.moveaxis(t, (1, 3), (0, 2))


def _from_chunks(t):
    nc, B, H, L, d = t.shape
    return jnp.moveaxis(t, (0, 2), (1, 3)).reshape(B, nc * L, H, d)


def rg_lru(x, w_a, b_a, w_x, b_x, lam):
    B, S, W = x.shape
    xb = x.reshape(B, S, LRU_BLOCKS, W // LRU_BLOCKS)
    r = jax.nn.sigmoid(jnp.einsum('bsni,nij->bsnj', xb, w_a).reshape(B, S, W) + b_a)
    i = jax.nn.sigmoid(jnp.einsum('bsni,nij->bsnj', xb, w_x).reshape(B, S, W) + b_x)
    log_a = -LRU_C * r * jax.nn.softplus(-lam)
    a = jnp.exp(log_a)
    u = jnp.sqrt(-jnp.expm1(2.0 * log_a)) * (i * x)

    def combine(e1, e2):
        a1, b1 = e1
        a2, b2 = e2
        return a1 * a2, a2 * b1 + b2

    _, h = lax.associative_scan(combine, (a, u), axis=1)
    return h


def mlstm_chunkwise(q, k, v, log_i, log_f):
    B, S, H, d = q.shape
    q = q * d ** -0.5
    mask = jnp.tril(jnp.ones((CHUNK, CHUNK), dtype=bool))

    def step(carry, xs):
        C, n, m = carry
        qc, kc, vc, ic, fc = xs
        b = jnp.cumsum(fc, axis=-1)
        Dm = jnp.where(mask, b[..., :, None] - b[..., None, :] + ic[..., None, :], -jnp.inf)
        inter = b + m[..., None]
        m_row = jnp.maximum(inter, jnp.max(Dm, axis=-1))
        w_inter = jnp.exp(inter - m_row)
        s = jnp.einsum('bhtd,bhsd->bhts', qc, kc) * jnp.exp(Dm - m_row[..., None])
        num = jnp.einsum('bhts,bhse->bhte', s, vc) + w_inter[..., None] * jnp.einsum('bhtd,bhde->bhte', qc, C)
        den = jnp.sum(s, axis=-1) + w_inter * jnp.einsum('bhtd,bhd->bht', qc, n)
        h = num / jnp.maximum(jnp.abs(den), jnp.exp(-m_row))[..., None]
        b_last = b[..., -1]
        w_log = b_last[..., None] - b + ic
        m_new = jnp.maximum(b_last + m, jnp.max(w_log, axis=-1))
        decay = jnp.exp(b_last + m - m_new)
        w_s = jnp.exp(w_log - m_new[..., None])
        C = decay[..., None, None] * C + jnp.einsum('bhs,bhsd,bhse->bhde', w_s, kc, vc)
        n = decay[..., None] * n + jnp.einsum('bhs,bhsd->bhd', w_s, kc)
        return (C, n, m_new), h

    init = (jnp.zeros((B, H, d, d), jnp.float32), jnp.zeros((B, H, d), jnp.float32),
            jnp.zeros((B, H), jnp.float32))
    xs = tuple(_to_chunks(t) for t in (q, k, v, log_i, log_f))
    _, h = lax.scan(step, init, xs)
    return _from_chunks(h)


def gla_chunkwise(q, k, v, log_g):
    B, S, H, dk = q.shape
    dv = v.shape[-1]
    q = q * dk ** -0.5
    mask = jnp.tril(jnp.ones((CHUNK, CHUNK), dtype=bool))

    def step(state, xs):
        qc, kc, vc, gc = xs
        G = jnp.cumsum(gc, axis=2)
        diff = jnp.where(mask[:, :, None], G[:, :, :, None, :] - G[:, :, None, :, :], -jnp.inf)
        A = jnp.einsum('bhtd,bhsd,bhtsd->bhts', qc, kc, jnp.exp(diff))
        o = jnp.einsum('bhts,bhse->bhte', A, vc) + jnp.einsum('bhtd,bhde->bhte', qc * jnp.exp(G), state)
        G_last = G[:, :, -1:, :]
        state = (jnp.exp(G_last[:, :, 0, :, None]) * state
                 + jnp.einsum('bhsd,bhse->bhde', kc * jnp.exp(G_last - G), vc))
        return state, o

    init = jnp.zeros((B, H, dk, dv), jnp.float32)
    xs = tuple(_to_chunks(t) for t in (q, k, v, log_g))
    _, o = lax.scan(step, init, xs)
    return _from_chunks(o)


def hybrid_mixer(h, w_in, lru_conv_w, lru_conv_b, lru_wa, lru_ba, lru_wx, lru_bx, lru_lambda,
                 m_conv_w, m_conv_b, m_wq, m_wk, m_wv, m_wi, m_bi, m_wf, m_bf, m_norm_w, m_skip,
                 h_lb, h_norm_w, g_w_lr2, g_b_lr2, g_norm_w, w_branch, w_out):
    B, S, _ = h.shape
    u = jnp.einsum('bsd,de->bse', h, w_in).astype(jnp.float32)
    idx = np.cumsum(np.array(SPLIT_SIZES))[:-1].tolist()
    (lru_x, lru_z, m_x, m_o, m_z, h_q, h_f, h_i, h_z,
     g_q, g_k, g_v, g_lr, g_z, merge) = jnp.split(u, idx, axis=-1)

    xa = causal_conv(lru_x, lru_conv_w, lru_conv_b)
    y_a = rg_lru(xa, lru_wa, lru_ba, lru_wx, lru_bx, lru_lambda) * jax.nn.silu(lru_z)

    xm = jax.nn.silu(causal_conv(m_x, m_conv_w, m_conv_b))
    q = headwise(xm, m_wq)
    k = headwise(xm, m_wk)
    v = headwise(m_x, m_wv)
    qkv = jnp.concatenate([q, k, v], axis=-1)
    log_i = qkv @ m_wi + m_bi
    log_f = jax.nn.log_sigmoid(qkv @ m_wf + m_bf)
    hm = mlstm_chunkwise(split_heads(q, MLSTM_HEADS), split_heads(k, MLSTM_HEADS),
                         split_heads(v, MLSTM_HEADS), log_i, log_f).reshape(B, S, W_BRANCH)
    hm = jax.nn.sigmoid(m_o) * hm
    y_b = (head_layernorm(hm, m_norm_w, MLSTM_HEADS) + m_skip * xm) * jax.nn.silu(m_z)

    hq = jax.nn.silu(h_q)
    log_fh = jnp.logaddexp(jnp.log(h_lb), jnp.log1p(-h_lb) + jax.nn.log_sigmoid(h_f))
    hk = (1.0 - h_lb) * jax.nn.sigmoid(-h_f)
    oh = gla_chunkwise(split_heads(hq, HGRN_HEADS), split_heads(hk, HGRN_HEADS),
                       split_heads(h_i, HGRN_HEADS), split_heads(log_fh, HGRN_HEADS))
    y_c = head_rmsnorm(oh, h_norm_w) * jax.nn.silu(h_z)

    log_gk = jax.nn.log_sigmoid(g_lr @ g_w_lr2 + g_b_lr2) / GLA_GATE_TAU
    og = gla_chunkwise(split_heads(g_q, GLA_HEADS), split_heads(g_k, GLA_HEADS),
                       split_heads(g_v, GLA_HEADS), split_heads(log_gk, GLA_HEADS))
    y_d = head_rmsnorm(og, g_norm_w) * jax.nn.silu(g_z)

    branches = jnp.stack([y_a, y_b, y_c, y_d], axis=2)
    proj = jnp.einsum('bsnw,nwd->bsnd', branches, w_branch)
    gates = jax.nn.sigmoid(merge.reshape(B, S, N_BRANCH, D_MODEL))
    merged = jnp.sum(gates * proj, axis=2)
    return merged @ w_out


def setup_inputs(seed: int = 0) -> dict:
    key = jax.random.key(seed)
    ks = jax.random.split(key, 32)

    def nrm(k, shape, scale):
        return jax.random.normal(k, shape, jnp.float32) * scale

    W = W_BRANCH
    nb_m = W // MLSTM_QKV_BLOCK
    u = jax.random.uniform(ks[7], (DEPTH, W), jnp.float32, minval=0.9, maxval=0.999)
    base = u ** (1.0 / LRU_C)
    lam = jnp.log(base / (1.0 - base))
    return {
        "x": nrm(ks[0], (BATCH, SEQ, D_MODEL), 1.0),
        "norm_g": 1.0 + nrm(ks[1], (DEPTH, D_MODEL), 0.02),
        "w_in": nrm(ks[2], (DEPTH, D_MODEL, D_IN), D_MODEL ** -0.5),
        "lru_conv_w": nrm(ks[3], (DEPTH, CONV_WIDTH, W), CONV_WIDTH ** -0.5),
        "lru_conv_b": nrm(ks[4], (DEPTH, W), 0.01),
        "lru_wa": nrm(ks[5], (DEPTH, LRU_BLOCKS, W // LRU_BLOCKS, W // LRU_BLOCKS), (W // LRU_BLOCKS) ** -0.5),
        "lru_ba": nrm(ks[6], (DEPTH, W), 0.01),
        "lru_wx": nrm(ks[8], (DEPTH, LRU_BLOCKS, W // LRU_BLOCKS, W // LRU_BLOCKS), (W // LRU_BLOCKS) ** -0.5),
        "lru_bx": nrm(ks[9], (DEPTH, W), 0.01),
        "lru_lambda": lam,
        "m_conv_w": nrm(ks[10], (DEPTH, CONV_WIDTH, W), CONV_WIDTH ** -0.5),
        "m_conv_b": nrm(ks[11], (DEPTH, W), 0.01),
        "m_wq": nrm(ks[12], (DEPTH, nb_m, MLSTM_QKV_BLOCK, MLSTM_QKV_BLOCK), MLSTM_QKV_BLOCK ** -0.5),
        "m_wk": nrm(ks[13], (DEPTH, nb_m, MLSTM_QKV_BLOCK, MLSTM_QKV_BLOCK), MLSTM_QKV_BLOCK ** -0.5),
        "m_wv": nrm(ks[14], (DEPTH, nb_m, MLSTM_QKV_BLOCK, MLSTM_QKV_BLOCK), MLSTM_QKV_BLOCK ** -0.5),
        "m_wi": nrm(ks[15], (DEPTH, 3 * W, MLSTM_HEADS), 0.01),
        "m_bi": nrm(ks[16], (DEPTH, MLSTM_HEADS), 0.1),
        "m_wf": nrm(ks[17], (DEPTH, 3 * W, MLSTM_HEADS), 0.01),
        "m_bf": jnp.broadcast_to(jnp.linspace(3.0, 6.0, MLSTM_HEADS, dtype=jnp.float32), (DEPTH, MLSTM_HEADS)) + nrm(ks[18], (DEPTH, MLSTM_HEADS), 0.01),
        "m_norm_w": 1.0 + nrm(ks[19], (DEPTH, W), 0.02),
        "m_skip": 1.0 + nrm(ks[20], (DEPTH, W), 0.02),
        "h_lb_logits": nrm(ks[21], (DEPTH, W), 0.5),
        "h_norm_w": 1.0 + nrm(ks[22], (DEPTH, W // HGRN_HEADS), 0.02),
        "g_w_lr2": nrm(ks[23], (DEPTH, GLA_GATE_RANK, GLA_DK), GLA_GATE_RANK ** -0.5),
        "g_b_lr2": nrm(ks[24], (DEPTH, GLA_DK), 0.01),
        "g_norm_w": 1.0 + nrm(ks[25], (DEPTH, GLA_DV // GLA_HEADS), 0.02),
        "w_branch": nrm(ks[26], (DEPTH, N_BRANCH, W, D_MODEL), W ** -0.5),
        "w_out": nrm(ks[27], (DEPTH, D_MODEL, D_MODEL), D_MODEL ** -0.5),
        "final_g": 1.0 + nrm(ks[28], (D_MODEL,), 0.02),
    }


def reference(x, norm_g, w_in, lru_conv_w, lru_conv_b, lru_wa, lru_ba, lru_wx, lru_bx, lru_lambda,
              m_conv_w, m_conv_b, m_wq, m_wk, m_wv, m_wi, m_bi, m_wf, m_bf, m_norm_w, m_skip,
              h_lb_logits, h_norm_w, g_w_lr2, g_b_lr2, g_norm_w, w_branch, w_out, final_g):
    lb_all = jnp.cumsum(jax.nn.softmax(h_lb_logits.astype(jnp.float32), axis=0), axis=0)
    lb_all = lb_all - lb_all[0]
    for l in range(DEPTH):
        h = rmsnorm(x, norm_g[l])
        x = x + hybrid_mixer(h, w_in[l], lru_conv_w[l], lru_conv_b[l], lru_wa[l], lru_ba[l],
                             lru_wx[l], lru_bx[l], lru_lambda[l], m_conv_w[l], m_conv_b[l],
                             m_wq[l], m_wk[l], m_wv[l], m_wi[l], m_bi[l], m_wf[l], m_bf[l],
                             m_norm_w[l], m_skip[l], lb_all[l], h_norm_w[l], g_w_lr2[l],
                             g_b_lr2[l], g_norm_w[l], w_branch[l], w_out[l])
    return rmsnorm(x, final_g)
```

```python
import functools

import numpy as np
import jax
import jax.numpy as jnp
from jax import lax
from jax.experimental import pallas as pl
from jax.experimental.pallas import tpu as pltpu

D_MODEL = 1024
W_BRANCH = 512
N_BRANCH = 4
CONV_WIDTH = 4
LRU_BLOCKS = 8
LRU_C = 8.0
N_HEADS = 4
HEAD_DIM = W_BRANCH // N_HEADS
QKV_BLOCK = 4
GLA_DK = 256
GLA_HEAD_DK = GLA_DK // N_HEADS
GLA_RANK = 16
GLA_TAU = 16.0
NORM_EPS = 1e-6
SPLIT_SIZES = (512, 512, 512, 512, 512, 512, 512, 512, 512, 256, 256, 512, 16, 512, 4096)

LANES = 128
SUBLANES = 8
MXU_DIM = 256
VMEM_LIMIT_BYTES = 56 * 1024 * 1024

TILE = 256

COL_A = 0
COL_B = COL_A + 2 * W_BRANCH
COL_C = COL_B + 3 * W_BRANCH
COL_D = COL_C + 4 * W_BRANCH
COL_M = COL_D + 4 * W_BRANCH + LANES
COL_END = COL_M + N_BRANCH * D_MODEL

ROW_LRU_CONV_W = 0
ROW_LRU_CONV_B = 4
ROW_LRU_BA = 5
ROW_LRU_BX = 6
ROW_LRU_LAMBDA = 7
ROW_M_CONV_W = 8
ROW_M_CONV_B = 12
ROW_M_NORM_W = 13
ROW_M_SKIP = 14
ROW_H_NORM_W = 15
ROW_G_NORM_W = 16
ROW_G_B_LR2 = 17
N_VEC_ROWS = 24

BF16 = jnp.bfloat16
F32 = jnp.float32


def _dot(a, b):
    return jnp.dot(a, b, preferred_element_type=F32)


def _dot_nt(a, b):
    return lax.dot_general(a, b, (((1,), (1,)), ((), ())), preferred_element_type=F32)


def _dot_tn(a, b):
    return lax.dot_general(a, b, (((0,), (0,)), ((), ())), preferred_element_type=F32)


def _sigmoid(x):
    return jax.nn.sigmoid(x)


def _silu(x):
    return x * _sigmoid(x)


def _log_sigmoid(x):
    return jnp.minimum(x, 0.0) - jnp.log1p(jnp.exp(-jnp.abs(x)))


def _softplus(x):
    return jnp.maximum(x, 0.0) + jnp.log1p(jnp.exp(-jnp.abs(x)))


def _row_iota(shape):
    return lax.broadcasted_iota(jnp.int32, shape, 0)


def _block_ref(x, b):
    t, c = x.shape
    if b == 1:
        rows = _row_iota(x.shape)
        return jnp.where((rows & 1) == 1, pltpu.roll(x, 1, 0), x)
    if b == 2:
        m = _row_iota(x.shape) & 3
        return jnp.where(m == 0, pltpu.roll(x, t - 1, 0),
                         jnp.where(m == 1, x, jnp.where(m == 2, pltpu.roll(x, 1, 0), pltpu.roll(x, 2, 0))))
    nb = t // (2 * b)
    x3 = x.reshape(nb, 2 * b, c)
    return jnp.broadcast_to(x3[:, b - 1:b, :], (nb, 2 * b, c)).reshape(t, c)


def _prefix_scan_rows(x, combine):
    t = x.shape[0]
    rows = _row_iota(x.shape)
    b = 1
    while b < t:
        x = jnp.where((rows & b) != 0, combine(_block_ref(x, b), x), x)
        b *= 2
    return x


def _linear_recurrence_rows(a, u):
    t = a.shape[0]
    rows = _row_iota(a.shape)
    b = 1
    while b < t:
        upper = (rows & b) != 0
        a_ref = _block_ref(a, b)
        u_ref = _block_ref(u, b)
        u = jnp.where(upper, a * u_ref + u, u)
        a = jnp.where(upper, a * a_ref, a)
        b *= 2
    return a, u


def _causal_conv(x, buf_ref, w, bias):
    t = x.shape[0]
    buf_ref[SUBLANES:SUBLANES + t, :] = x
    y = bias + w[CONV_WIDTH - 1] * x
    for back in range(1, CONV_WIDTH):
        y = y + w[CONV_WIDTH - 1 - back] * buf_ref[SUBLANES - back:SUBLANES - back + t, :]
    buf_ref[0:SUBLANES, :] = buf_ref[t:t + SUBLANES, :]
    return y


def _head_slices():
    return [slice(h * HEAD_DIM, (h + 1) * HEAD_DIM) for h in range(N_HEADS)]


def _gla_chunk(q, k, v, g, st_ref, lvl, scale):
    t = q.shape[0]
    heads = _head_slices()
    rows = _row_iota(q.shape)
    q = q * scale
    vb = v.astype(BF16)
    qb = q.astype(BF16)
    kb = k.astype(BF16)
    scores = [jnp.where(lvl == 0, _dot_nt(qb[:, sl], kb[:, sl]), 0.0) for sl in heads]
    p = g
    b, level = 1, 1
    while b < t:
        ref = _block_ref(p, b)
        upper = (rows & b) != 0
        e = jnp.exp(jnp.where(upper, p, ref - p))
        qe = (q * e).astype(BF16)
        ke = (k * e).astype(BF16)
        scores = [jnp.where(lvl == level, _dot_nt(qe[:, sl], ke[:, sl]), s) for sl, s in zip(heads, scores)]
        p = jnp.where(upper, p + ref, p)
        b *= 2
        level += 1
    g_last = p[t - 1:t, :]
    q_in = (q * jnp.exp(p)).astype(BF16)
    k_out = (k * jnp.exp(g_last - p)).astype(BF16)
    state_decay = jnp.exp(g_last)
    outs = []
    for h, sl in enumerate(heads):
        st = st_ref[h]
        outs.append(_dot(scores[h].astype(BF16), vb[:, sl]) + _dot_nt(q_in[:, sl], st.astype(BF16)))
        st_ref[h] = st * state_decay[:, sl] + _dot_tn(vb[:, sl], k_out[:, sl])
    return jnp.concatenate(outs, axis=1)


def _mlstm_chunk(q, k, v, log_i, log_f, c_ref, n_ref, m_ref, lvl):
    t = q.shape[0]
    heads = _head_slices()
    b = _prefix_scan_rows(log_f, lambda lo, hi: lo + hi)
    z = log_i - b
    zmax = _prefix_scan_rows(z, jnp.maximum)
    m_prev = m_ref[0:1, :]
    m_t = b + jnp.maximum(m_prev, zmax)
    col = b - m_t
    w_inter = jnp.exp(b + m_prev - m_t)
    inv_floor = jnp.exp(-m_t)
    z_rows = z.T
    b_last = b[t - 1:t, :]
    m_new = m_t[t - 1:t, :]
    w_state = jnp.exp(z + b_last - m_new)
    decay = jnp.exp(b_last + m_prev - m_new)
    qs = q * (HEAD_DIM ** -0.5)
    qb = qs.astype(BF16)
    kb = k.astype(BF16)
    vb = v.astype(BF16)
    n_prev = n_ref[0:1, :]
    outs, n_new = [], []
    for h, sl in enumerate(heads):
        hl = slice(h, h + 1)
        dm = col[:, hl] + z_rows[hl, :]
        s = _dot_nt(qb[:, sl], kb[:, sl]) * jnp.exp(jnp.where(lvl >= 0, dm, -jnp.inf))
        c_h = c_ref[h]
        wi = w_inter[:, hl]
        num = _dot(s.astype(BF16), vb[:, sl]) + wi * _dot(qb[:, sl], c_h.astype(BF16))
        den = (jnp.sum(s, axis=-1, keepdims=True)
               + wi * jnp.sum(qs[:, sl] * n_prev[:, sl], axis=-1, keepdims=True))
        outs.append(num / jnp.maximum(jnp.abs(den), inv_floor[:, hl]))
        kw = k[:, sl] * w_state[:, hl]
        d_h = decay[:, hl]
        c_ref[h] = d_h * c_h + _dot_tn(kw.astype(BF16), vb[:, sl])
        n_new.append(d_h * n_prev[:, sl] + jnp.sum(kw, axis=0, keepdims=True))
    n_ref[0:1, :] = jnp.concatenate(n_new, axis=1)
    m_ref[0:1, :] = m_new
    return jnp.concatenate(outs, axis=1)


def _per_head(x, fn):
    return jnp.concatenate([fn(x[:, sl]) for sl in _head_slices()], axis=1)


def _rms_normalize(x):
    return x * lax.rsqrt(jnp.mean(x * x, axis=-1, keepdims=True) + NORM_EPS)


def _layer_normalize(x):
    xc = x - jnp.mean(x, axis=-1, keepdims=True)
    return xc * lax.rsqrt(jnp.mean(xc * xc, axis=-1, keepdims=True) + NORM_EPS)


def _layer_kernel(layer, final_norm,
                  x_ref, ng_ref, fg_ref, w_ref, wb_ref, wo_ref, lruw_ref, mqk_ref, mv_ref, wg_ref, bg_ref,
                  w2_ref, vec_ref, lbl_ref, lvl_ref,
                  o_ref,
                  conv_a, conv_b, lru_h, m_c, m_n, m_m, h_s, g_s):
    half = MXU_DIM

    @pl.when(pl.program_id(1) == 0)
    def _reset_state():
        conv_a[0:SUBLANES, :] = jnp.zeros((SUBLANES, W_BRANCH), F32)
        conv_b[0:SUBLANES, :] = jnp.zeros((SUBLANES, W_BRANCH), F32)
        lru_h[...] = jnp.zeros(lru_h.shape, F32)
        m_c[...] = jnp.zeros(m_c.shape, F32)
        m_n[...] = jnp.zeros(m_n.shape, F32)
        m_m[...] = jnp.zeros(m_m.shape, F32)
        h_s[...] = jnp.zeros(h_s.shape, F32)
        g_s[...] = jnp.zeros(g_s.shape, F32)

    def vec(row):
        return vec_ref[row:row + 1, :]

    lvl = lvl_ref[...]
    x = x_ref[...]
    hb = (_rms_normalize(x) * ng_ref[...]).astype(BF16)

    def in_proj(lo, hi):
        return _dot(hb, w_ref[:, lo:hi])

    def merge_term(branch, y):
        gate = _sigmoid(in_proj(COL_M + branch * D_MODEL, COL_M + (branch + 1) * D_MODEL))
        return gate * _dot(y.astype(BF16), wb_ref[branch])

    u = in_proj(COL_A, COL_B)
    lru_x, lru_z = u[:, :W_BRANCH], u[:, W_BRANCH:]
    xa = _causal_conv(lru_x, conv_a, [vec(ROW_LRU_CONV_W + j) for j in range(CONV_WIDTH)], vec(ROW_LRU_CONV_B))
    xab = xa.astype(BF16)
    g0 = _dot(xab[:, :half], lruw_ref[0])
    g1 = _dot(xab[:, half:], lruw_ref[1])
    r = _sigmoid(jnp.concatenate([g0[:, :half], g1[:, :half]], axis=1) + vec(ROW_LRU_BA))
    i = _sigmoid(jnp.concatenate([g0[:, half:], g1[:, half:]], axis=1) + vec(ROW_LRU_BX))
    log_a = (-LRU_C) * r * _softplus(-vec(ROW_LRU_LAMBDA))
    a = jnp.exp(log_a)
    a_cum, h_loc = _linear_recurrence_rows(a, jnp.sqrt(1.0 - a * a) * (i * xa))
    h_lru = h_loc + a_cum * lru_h[SUBLANES - 1:SUBLANES, :]
    lru_h[...] = h_lru[TILE - SUBLANES:, :]
    merged = merge_term(0, h_lru * _silu(lru_z))

    u = in_proj(COL_B, COL_C)
    m_x, m_o, m_z = u[:, :W_BRANCH], u[:, W_BRANCH:2 * W_BRANCH], u[:, 2 * W_BRANCH:]
    xm = _silu(_causal_conv(m_x, conv_b, [vec(ROW_M_CONV_W + j) for j in range(CONV_WIDTH)], vec(ROW_M_CONV_B)))
    xmb = xm.astype(BF16)
    mxb = m_x.astype(BF16)
    qk0 = _dot(xmb[:, :half], mqk_ref[0])
    qk1 = _dot(xmb[:, half:], mqk_ref[1])
    q = jnp.concatenate([qk0[:, :half], qk1[:, :half]], axis=1)
    k = jnp.concatenate([qk0[:, half:], qk1[:, half:]], axis=1)
    v = jnp.concatenate([_dot(mxb[:, :half], mv_ref[0]), _dot(mxb[:, half:], mv_ref[1])], axis=1)
    gates = _dot(jnp.concatenate([q, k, v], axis=1).astype(BF16), wg_ref[...]) + bg_ref[...]
    hm = _mlstm_chunk(q, k, v, gates[:, :LANES], _log_sigmoid(gates[:, LANES:]), m_c, m_n, m_m, lvl)
    hm = _sigmoid(m_o) * hm
    y_b = (_per_head(hm, _layer_normalize) * vec(ROW_M_NORM_W) + vec(ROW_M_SKIP) * xm) * _silu(m_z)
    merged = merged + merge_term(1, y_b)

    u = in_proj(COL_C, COL_D)
    h_q, h_f = u[:, :W_BRANCH], u[:, W_BRANCH:2 * W_BRANCH]
    h_i, h_z = u[:, 2 * W_BRANCH:3 * W_BRANCH], u[:, 3 * W_BRANCH:]
    logits = lbl_ref[...]
    ex = jnp.exp(logits - jnp.max(logits, axis=0, keepdims=True))
    sm = ex / jnp.sum(ex, axis=0, keepdims=True)
    lb = jnp.zeros((1, W_BRANCH), F32)
    for j in range(1, layer + 1):
        lb = lb + sm[j:j + 1, :]
    e_f = jnp.exp(-jnp.abs(h_f))
    r_f = 1.0 / (1.0 + e_f)
    sig_neg = jnp.where(h_f >= 0, e_f * r_f, r_f)
    lo = jnp.log(lb)
    hi = jnp.log1p(-lb) + (jnp.minimum(h_f, 0.0) - jnp.log1p(e_f))
    log_fh = jnp.maximum(lo, hi) + jnp.log1p(jnp.exp(-jnp.abs(lo - hi)))
    oh = _gla_chunk(_silu(h_q), (1.0 - lb) * sig_neg, h_i, log_fh, h_s, lvl, HEAD_DIM ** -0.5)
    y_c = _per_head(oh, _rms_normalize) * vec(ROW_H_NORM_W) * _silu(h_z)
    merged = merged + merge_term(2, y_c)

    u = in_proj(COL_D, COL_M)
    g_q, g_k, g_v = u[:, :W_BRANCH], u[:, W_BRANCH:2 * W_BRANCH], u[:, 2 * W_BRANCH:3 * W_BRANCH]
    g_lr, g_z = u[:, 3 * W_BRANCH:3 * W_BRANCH + LANES], u[:, 3 * W_BRANCH + LANES:]
    log_gk = _log_sigmoid(_dot(g_lr.astype(BF16), w2_ref[...]) + vec(ROW_G_B_LR2)) * (1.0 / GLA_TAU)
    og = _gla_chunk(g_q, g_k, g_v, log_gk, g_s, lvl, GLA_HEAD_DK ** -0.5)
    y_d = _per_head(og, _rms_normalize) * vec(ROW_G_NORM_W) * _silu(g_z)
    merged = merged + merge_term(3, y_d)

    y = x + _dot(merged.astype(BF16), wo_ref[...])
    if final_norm:
        y = _rms_normalize(y) * fg_ref[...]
    o_ref[...] = y


def _block_diag(w):
    n, b, _ = w.shape
    return jnp.einsum('nij,nm->nimj', w, jnp.eye(n, dtype=w.dtype)).reshape(n * b, n * b)


def _diag_tiles(dense):
    return jnp.stack([dense[i * MXU_DIM:(i + 1) * MXU_DIM, i * MXU_DIM:(i + 1) * MXU_DIM]
                      for i in range(W_BRANCH // MXU_DIM)])


def _pad_key_heads(w):
    lead = w.shape[:-1]
    w = w.reshape(lead + (N_HEADS, GLA_HEAD_DK))
    w = jnp.pad(w, [(0, 0)] * len(lead) + [(0, 0), (0, HEAD_DIM - GLA_HEAD_DK)])
    return w.reshape(lead + (N_HEADS * HEAD_DIM,))


def _level_table(t):
    idx = np.arange(t)
    xor = idx[:, None] ^ idx[None, :]
    lv = np.floor(np.log2(np.maximum(xor, 1))).astype(np.int32) + 1
    lv = np.where(idx[:, None] == idx[None, :], 0, lv)
    return np.where(idx[:, None] >= idx[None, :], lv, -1).astype(np.int32)


def _pack_layer(l, w_in, lru_conv_w, lru_conv_b, lru_wa, lru_ba, lru_wx, lru_bx, lru_lambda,
                m_conv_w, m_conv_b, m_wq, m_wk, m_wv, m_wi, m_bi, m_wf, m_bf, m_norm_w, m_skip,
                h_norm_w, g_w_lr2, g_b_lr2, g_norm_w, w_branch, w_out):
    offs = np.concatenate([[0], np.cumsum(SPLIT_SIZES)])
    cols = [w_in[l][:, offs[i]:offs[i + 1]] for i in range(len(SPLIT_SIZES))]
    (lru_x, lru_z, m_x, m_o, m_z, h_q, h_f, h_i, h_z, g_q, g_k, g_v, g_lr, g_z, merge) = cols
    g_lr = jnp.pad(g_lr, ((0, 0), (0, LANES - GLA_RANK)))
    w_all = jnp.concatenate([lru_x, lru_z, m_x, m_o, m_z, h_q, h_f, h_i, h_z,
                             _pad_key_heads(g_q), _pad_key_heads(g_k), g_v, g_lr, g_z, merge], axis=1).astype(BF16)
    lru_w = jnp.concatenate([_diag_tiles(_block_diag(lru_wa[l])), _diag_tiles(_block_diag(lru_wx[l]))], axis=2)
    m_qk = jnp.concatenate([_diag_tiles(_block_diag(m_wq[l])), _diag_tiles(_block_diag(m_wk[l]))], axis=2)
    m_v = _diag_tiles(_block_diag(m_wv[l]))
    w_gate = jnp.zeros((3 * W_BRANCH, 2 * LANES), F32)
    w_gate = w_gate.at[:, :N_HEADS].set(m_wi[l]).at[:, LANES:LANES + N_HEADS].set(m_wf[l])
    b_gate = jnp.zeros((1, 2 * LANES), F32)
    b_gate = b_gate.at[0, :N_HEADS].set(m_bi[l]).at[0, LANES:LANES + N_HEADS].set(m_bf[l])
    w_lr2 = jnp.pad(_pad_key_heads(g_w_lr2[l]), ((0, LANES - GLA_RANK), (0, 0)))
    rows = [lru_conv_w[l][j] for j in range(CONV_WIDTH)]
    rows += [lru_conv_b[l], lru_ba[l], lru_bx[l], lru_lambda[l]]
    rows += [m_conv_w[l][j] for j in range(CONV_WIDTH)]
    rows += [m_conv_b[l], m_norm_w[l], m_skip[l], jnp.tile(h_norm_w[l], N_HEADS), jnp.tile(g_norm_w[l], N_HEADS),
             _pad_key_heads(g_b_lr2[l])]
    vecs = jnp.stack(rows).astype(F32)
    vecs = jnp.pad(vecs, ((0, N_VEC_ROWS - vecs.shape[0]), (0, 0)))
    return (w_all, w_branch[l].astype(BF16), w_out[l].astype(BF16), lru_w.astype(BF16), m_qk.astype(BF16),
            m_v.astype(BF16), w_gate.astype(BF16), b_gate, w_lr2.astype(BF16), vecs)


def _layer_call(layer, final_norm, batch, seq):
    n_tiles = seq // TILE
    tile_spec = pl.BlockSpec((TILE, D_MODEL), lambda b, j: (b * n_tiles + j, 0))
    resident = pl.BlockSpec(memory_space=pltpu.VMEM)
    state = functools.partial(pltpu.VMEM, dtype=F32)
    return pl.pallas_call(
        functools.partial(_layer_kernel, layer, final_norm),
        grid=(batch, n_tiles),
        in_specs=[tile_spec] + [resident] * 14,
        out_specs=tile_spec,
        out_shape=jax.ShapeDtypeStruct((batch * seq, D_MODEL), F32),
        scratch_shapes=[
            state((TILE + SUBLANES, W_BRANCH)),
            state((TILE + SUBLANES, W_BRANCH)),
            state((SUBLANES, W_BRANCH)),
            state((N_HEADS, HEAD_DIM, HEAD_DIM)),
            state((SUBLANES, W_BRANCH)),
            state((SUBLANES, LANES)),
            state((N_HEADS, HEAD_DIM, HEAD_DIM)),
            state((N_HEADS, HEAD_DIM, HEAD_DIM)),
        ],
        compiler_params=pltpu.CompilerParams(
            dimension_semantics=("arbitrary", "arbitrary"),
            vmem_limit_bytes=VMEM_LIMIT_BYTES,
        ),
        name=f"hybrid_layer_{layer}",
    )


def kernel(x, norm_g, w_in, lru_conv_w, lru_conv_b, lru_wa, lru_ba, lru_wx, lru_bx, lru_lambda, m_conv_w, m_conv_b, m_wq, m_wk, m_wv, m_wi, m_bi, m_wf, m_bf, m_norm_w, m_skip, h_lb_logits, h_norm_w, g_w_lr2, g_b_lr2, g_norm_w, w_branch, w_out, final_g):
    batch, seq, d = x.shape
    depth = w_in.shape[0]
    assert d == D_MODEL and seq % TILE == 0
    lvl = jnp.asarray(_level_table(TILE))
    lb_logits = h_lb_logits.astype(F32)
    y = x.reshape(batch * seq, d).astype(F32)
    for l in range(depth):
        packed = _pack_layer(l, w_in, lru_conv_w, lru_conv_b, lru_wa, lru_ba, lru_wx, lru_bx, lru_lambda,
                             m_conv_w, m_conv_b, m_wq, m_wk, m_wv, m_wi, m_bi, m_wf, m_bf, m_norm_w, m_skip,
                             h_norm_w, g_w_lr2, g_b_lr2, g_norm_w, w_branch, w_out)
        (w_all, w_br, w_o, lru_w, m_qk, m_v, w_gate, b_gate, w_lr2, vecs) = packed
        y = _layer_call(l, l == depth - 1, batch, seq)(
            y, norm_g[l].reshape(1, d), final_g.reshape(1, d), w_all, w_br, w_o, lru_w, m_qk, m_v, w_gate, b_gate,
            w_lr2, vecs, lb_logits, lvl)
    return y.reshape(batch, seq, d)
```

```python
import functools

import numpy as np
import jax
import jax.numpy as jnp
from jax import lax
from jax.experimental import pallas as pl
from jax.experimental.pallas import tpu as pltpu

D_MODEL = 1024
W_BRANCH = 512
N_BRANCH = 4
CONV_WIDTH = 4
LRU_BLOCKS = 8
LRU_C = 8.0
N_HEADS = 4
HEAD_DIM = W_BRANCH // N_HEADS
QKV_BLOCK = 4
GLA_DK = 256
GLA_HEAD_DK = GLA_DK // N_HEADS
GLA_RANK = 16
GLA_TAU = 16.0
NORM_EPS = 1e-6
SPLIT_SIZES = (512, 512, 512, 512, 512, 512, 512, 512, 512, 256, 256, 512, 16, 512, 4096)

LANES = 128
SUBLANES = 8
MXU_DIM = 256
VMEM_LIMIT_BYTES = 56 * 1024 * 1024

TILE = 256

COL_A = 0
COL_B = COL_A + 2 * W_BRANCH
COL_C = COL_B + 3 * W_BRANCH
COL_D = COL_C + 4 * W_BRANCH
COL_M = COL_D + 4 * W_BRANCH + LANES
COL_END = COL_M + N_BRANCH * D_MODEL

ROW_LRU_CONV_W = 0
ROW_LRU_CONV_B = 4
ROW_LRU_BA = 5
ROW_LRU_BX = 6
ROW_LRU_LAMBDA = 7
ROW_M_CONV_W = 8
ROW_M_CONV_B = 12
ROW_M_NORM_W = 13
ROW_M_SKIP = 14
ROW_H_NORM_W = 15
ROW_G_NORM_W = 16
ROW_G_B_LR2 = 17
N_VEC_ROWS = 24

BF16 = jnp.bfloat16
F32 = jnp.float32


def _dot(a, b):
    return jnp.dot(a, b, preferred_element_type=F32)


def _dot_nt(a, b):
    return lax.dot_general(a, b, (((1,), (1,)), ((), ())), preferred_element_type=F32)


def _dot_tn(a, b):
    return lax.dot_general(a, b, (((0,), (0,)), ((), ())), preferred_element_type=F32)


def _sigmoid(x):
    return jax.nn.sigmoid(x)


def _silu(x):
    return x * _sigmoid(x)


def _log_sigmoid(x):
    return jnp.minimum(x, 0.0) - jnp.log(1.0 + jnp.exp(-jnp.abs(x)))


def _softplus(x):
    return jnp.maximum(x, 0.0) + jnp.log(1.0 + jnp.exp(-jnp.abs(x)))


def _to_tiles(x):
    t, c = x.shape
    return x.reshape(t // SUBLANES, SUBLANES, c)


def _from_tiles(x3):
    nb, s, c = x3.shape
    return x3.reshape(nb * s, c)


def _sub_iota(shape):
    return lax.broadcasted_iota(jnp.int32, shape, 1)


def _block_ref(x, b):
    t, c = x.shape
    if b == 1:
        x3 = _to_tiles(x)
        return _from_tiles(jnp.where((_sub_iota(x3.shape) & 1) == 1, pltpu.roll(x3, 1, 1), x3))
    if b == 2:
        x3 = _to_tiles(x)
        m = _sub_iota(x3.shape) & 3
        return _from_tiles(jnp.where(m == 0, pltpu.roll(x3, SUBLANES - 1, 1),
                                     jnp.where(m == 1, x3,
                                               jnp.where(m == 2, pltpu.roll(x3, 1, 1), pltpu.roll(x3, 2, 1)))))
    nb = t // (2 * b)
    x3 = x.reshape(nb, 2 * b, c)
    return jnp.broadcast_to(x3[:, b - 1:b, :], (nb, 2 * b, c)).reshape(t, c)


def _prefix_scan_rows(x, combine, identity):
    x3 = _to_tiles(x)
    nb, _, c = x3.shape
    j = _sub_iota(x3.shape)
    for k in (1, 2, 4):
        x3 = combine(jnp.where(j >= k, pltpu.roll(x3, k, 1), identity), x3)
    totals = x3[:, SUBLANES - 1:, :]
    carry = jnp.full((1, 1, c), identity, x.dtype)
    carries = []
    for i in range(nb):
        carries.append(carry)
        carry = combine(carry, totals[i:i + 1])
    return _from_tiles(combine(jnp.concatenate(carries, axis=0), x3))


def _linear_recurrence_rows(a, u, h_prev):
    a3, u3 = _to_tiles(a), _to_tiles(u)
    nb, _, c = a3.shape
    j = _sub_iota(a3.shape)
    for k in (1, 2, 4):
        a_back = jnp.where(j >= k, pltpu.roll(a3, k, 1), 1.0)
        u_back = jnp.where(j >= k, pltpu.roll(u3, k, 1), 0.0)
        u3 = a3 * u_back + u3
        a3 = a3 * a_back
    a_tot, u_tot = a3[:, SUBLANES - 1:, :], u3[:, SUBLANES - 1:, :]
    h_in = h_prev.reshape(1, 1, c)
    carries = []
    for i in range(nb):
        carries.append(h_in)
        h_in = u_tot[i:i + 1] + a_tot[i:i + 1] * h_in
    return _from_tiles(u3 + a3 * jnp.concatenate(carries, axis=0))


def _causal_conv(x, tail_ref, w, bias):
    t, c = x.shape
    x3 = _to_tiles(x)
    prev3 = jnp.concatenate([tail_ref[...].reshape(1, SUBLANES, c), x3[:-1]], axis=0)
    j = _sub_iota(x3.shape)
    y = bias + w[CONV_WIDTH - 1] * x
    for back in range(1, CONV_WIDTH):
        shifted = pltpu.roll(jnp.where(j >= SUBLANES - back, prev3, x3), back, 1)
        y = y + w[CONV_WIDTH - 1 - back] * _from_tiles(shifted)
    tail_ref[...] = x[t - SUBLANES:, :]
    return y


def _head_slices():
    return [slice(h * HEAD_DIM, (h + 1) * HEAD_DIM) for h in range(N_HEADS)]


def _gla_chunk(q, k, v, g, st_ref, lvl, scale):
    t = q.shape[0]
    heads = _head_slices()
    q = q * scale
    vb = v.astype(BF16)
    qb = q.astype(BF16)
    kb = k.astype(BF16)
    scores = [jnp.where(lvl == 0, _dot_nt(qb[:, sl], kb[:, sl]), 0.0) for sl in heads]
    g_cum = _prefix_scan_rows(g, jnp.add, 0.0)
    b, level = 1, 1
    while b < t:
        e = jnp.exp(-jnp.abs(g_cum - _block_ref(g_cum, b)))
        qe = (q * e).astype(BF16)
        ke = (k * e).astype(BF16)
        scores = [jnp.where(lvl == level, _dot_nt(qe[:, sl], ke[:, sl]), s) for sl, s in zip(heads, scores)]
        b *= 2
        level += 1
    g_last = g_cum[t - 1:t, :]
    q_in = (q * jnp.exp(g_cum)).astype(BF16)
    k_out = (k * jnp.exp(g_last - g_cum)).astype(BF16)
    state_decay = jnp.exp(g_last)
    outs = []
    for h, sl in enumerate(heads):
        st = st_ref[h]
        outs.append(_dot(scores[h].astype(BF16), vb[:, sl]) + _dot_nt(q_in[:, sl], st.astype(BF16)))
        st_ref[h] = st * state_decay[:, sl] + _dot_tn(vb[:, sl], k_out[:, sl])
    return jnp.concatenate(outs, axis=1)


def _mlstm_chunk(q, k, v, log_i, log_f, c_ref, n_ref, m_ref, lvl):
    t = q.shape[0]
    heads = _head_slices()
    b = _prefix_scan_rows(log_f, jnp.add, 0.0)
    z = log_i - b
    zmax = _prefix_scan_rows(z, jnp.maximum, -jnp.inf)
    m_prev = m_ref[0:1, :]
    m_t = b + jnp.maximum(m_prev, zmax)
    col = b - m_t
    w_inter = jnp.exp(b + m_prev - m_t)
    inv_floor = jnp.exp(-m_t)
    z_rows = z.T
    b_last = b[t - 1:t, :]
    m_new = m_t[t - 1:t, :]
    w_state = jnp.exp(z + b_last - m_new)
    decay = jnp.exp(b_last + m_prev - m_new)
    qs = q * (HEAD_DIM ** -0.5)
    qb = qs.astype(BF16)
    kb = k.astype(BF16)
    vb = v.astype(BF16)
    n_prev = n_ref[0:1, :]
    outs, n_new = [], []
    for h, sl in enumerate(heads):
        hl = slice(h, h + 1)
        dm = col[:, hl] + z_rows[hl, :]
        s = _dot_nt(qb[:, sl], kb[:, sl]) * jnp.exp(jnp.where(lvl >= 0, dm, -jnp.inf))
        c_h = c_ref[h]
        wi = w_inter[:, hl]
        num = _dot(s.astype(BF16), vb[:, sl]) + wi * _dot(qb[:, sl], c_h.astype(BF16))
        den = (jnp.sum(s, axis=-1, keepdims=True)
               + wi * jnp.sum(qs[:, sl] * n_prev[:, sl], axis=-1, keepdims=True))
        outs.append(num / jnp.maximum(jnp.abs(den), inv_floor[:, hl]))
        kw = k[:, sl] * w_state[:, hl]
        d_h = decay[:, hl]
        c_ref[h] = d_h * c_h + _dot_tn(kw.astype(BF16), vb[:, sl])
        n_new.append(d_h * n_prev[:, sl] + jnp.sum(kw, axis=0, keepdims=True))
    n_ref[0:1, :] = jnp.concatenate(n_new, axis=1)
    m_ref[0:1, :] = m_new
    return jnp.concatenate(outs, axis=1)


def _per_head(x, fn):
    return jnp.concatenate([fn(x[:, sl]) for sl in _head_slices()], axis=1)


def _rms_normalize(x):
    return x * lax.rsqrt(jnp.mean(x * x, axis=-1, keepdims=True) + NORM_EPS)


def _layer_normalize(x):
    xc = x - jnp.mean(x, axis=-1, keepdims=True)
    return xc * lax.rsqrt(jnp.mean(xc * xc, axis=-1, keepdims=True) + NORM_EPS)


def _layer_kernel(layer, final_norm,
                  x_ref, ng_ref, fg_ref, w_ref, wb_ref, wo_ref, lruw_ref, mqk_ref, mv_ref, wg_ref, bg_ref,
                  w2_ref, vec_ref, lbl_ref, lvl_ref,
                  o_ref,
                  conv_a, conv_b, lru_h, m_c, m_n, m_m, h_s, g_s):
    half = MXU_DIM

    @pl.when(pl.program_id(1) == 0)
    def _reset_state():
        conv_a[...] = jnp.zeros(conv_a.shape, F32)
        conv_b[...] = jnp.zeros(conv_b.shape, F32)
        lru_h[...] = jnp.zeros(lru_h.shape, F32)
        m_c[...] = jnp.zeros(m_c.shape, F32)
        m_n[...] = jnp.zeros(m_n.shape, F32)
        m_m[...] = jnp.zeros(m_m.shape, F32)
        h_s[...] = jnp.zeros(h_s.shape, F32)
        g_s[...] = jnp.zeros(g_s.shape, F32)

    def vec(row):
        return vec_ref[row:row + 1, :]

    lvl = lvl_ref[...]
    x = x_ref[...]
    hb = (_rms_normalize(x) * ng_ref[...]).astype(BF16)

    def in_proj(lo, hi):
        return _dot(hb, w_ref[:, lo:hi])

    def merge_term(branch, y):
        gate = _sigmoid(in_proj(COL_M + branch * D_MODEL, COL_M + (branch + 1) * D_MODEL))
        return gate * _dot(y.astype(BF16), wb_ref[branch])

    u = in_proj(COL_A, COL_B)
    lru_x, lru_z = u[:, :W_BRANCH], u[:, W_BRANCH:]
    xa = _causal_conv(lru_x, conv_a, [vec(ROW_LRU_CONV_W + j) for j in range(CONV_WIDTH)], vec(ROW_LRU_CONV_B))
    xab = xa.astype(BF16)
    g0 = _dot(xab[:, :half], lruw_ref[0])
    g1 = _dot(xab[:, half:], lruw_ref[1])
    r = _sigmoid(jnp.concatenate([g0[:, :half], g1[:, :half]], axis=1) + vec(ROW_LRU_BA))
    i = _sigmoid(jnp.concatenate([g0[:, half:], g1[:, half:]], axis=1) + vec(ROW_LRU_BX))
    log_a = (-LRU_C) * r * _softplus(-vec(ROW_LRU_LAMBDA))
    a = jnp.exp(log_a)
    h_lru = _linear_recurrence_rows(a, jnp.sqrt(1.0 - a * a) * (i * xa),
                                    lru_h[SUBLANES - 1:SUBLANES, :])
    lru_h[...] = h_lru[TILE - SUBLANES:, :]
    merged = merge_term(0, h_lru * _silu(lru_z))

    u = in_proj(COL_B, COL_C)
    m_x, m_o, m_z = u[:, :W_BRANCH], u[:, W_BRANCH:2 * W_BRANCH], u[:, 2 * W_BRANCH:]
    xm = _silu(_causal_conv(m_x, conv_b, [vec(ROW_M_CONV_W + j) for j in range(CONV_WIDTH)], vec(ROW_M_CONV_B)))
    xmb = xm.astype(BF16)
    mxb = m_x.astype(BF16)
    qk0 = _dot(xmb[:, :half], mqk_ref[0])
    qk1 = _dot(xmb[:, half:], mqk_ref[1])
    q = jnp.concatenate([qk0[:, :half], qk1[:, :half]], axis=1)
    k = jnp.concatenate([qk0[:, half:], qk1[:, half:]], axis=1)
    v = jnp.concatenate([_dot(mxb[:, :half], mv_ref[0]), _dot(mxb[:, half:], mv_ref[1])], axis=1)
    gates = _dot(jnp.concatenate([q, k, v], axis=1).astype(BF16), wg_ref[...]) + bg_ref[...]
    hm = _mlstm_chunk(q, k, v, gates[:, :LANES], _log_sigmoid(gates[:, LANES:]), m_c, m_n, m_m, lvl)
    hm = _sigmoid(m_o) * hm
    y_b = (_per_head(hm, _layer_normalize) * vec(ROW_M_NORM_W) + vec(ROW_M_SKIP) * xm) * _silu(m_z)
    merged = merged + merge_term(1, y_b)

    u = in_proj(COL_C, COL_D)
    h_q, h_f = u[:, :W_BRANCH], u[:, W_BRANCH:2 * W_BRANCH]
    h_i, h_z = u[:, 2 * W_BRANCH:3 * W_BRANCH], u[:, 3 * W_BRANCH:]
    logits = lbl_ref[...]
    ex = jnp.exp(logits - jnp.max(logits, axis=0, keepdims=True))
    sm = ex / jnp.sum(ex, axis=0, keepdims=True)
    lb = jnp.zeros((1, W_BRANCH), F32)
    for j in range(1, layer + 1):
        lb = lb + sm[j:j + 1, :]
    e_f = jnp.exp(-jnp.abs(h_f))
    r_f = 1.0 / (1.0 + e_f)
    sig_neg = jnp.where(h_f >= 0, e_f * r_f, r_f)
    lo = jnp.log(lb)
    hi = jnp.log1p(-lb) + (jnp.minimum(h_f, 0.0) - jnp.log(1.0 + e_f))
    log_fh = jnp.maximum(lo, hi) + jnp.log(1.0 + jnp.exp(-jnp.abs(lo - hi)))
    oh = _gla_chunk(_silu(h_q), (1.0 - lb) * sig_neg, h_i, log_fh, h_s, lvl, HEAD_DIM ** -0.5)
    y_c = _per_head(oh, _rms_normalize) * vec(ROW_H_NORM_W) * _silu(h_z)
    merged = merged + merge_term(2, y_c)

    u = in_proj(COL_D, COL_M)
    g_q, g_k, g_v = u[:, :W_BRANCH], u[:, W_BRANCH:2 * W_BRANCH], u[:, 2 * W_BRANCH:3 * W_BRANCH]
    g_lr, g_z = u[:, 3 * W_BRANCH:3 * W_BRANCH + LANES], u[:, 3 * W_BRANCH + LANES:]
    log_gk = _log_sigmoid(_dot(g_lr.astype(BF16), w2_ref[...]) + vec(ROW_G_B_LR2)) * (1.0 / GLA_TAU)
    og = _gla_chunk(g_q, g_k, g_v, log_gk, g_s, lvl, GLA_HEAD_DK ** -0.5)
    y_d = _per_head(og, _rms_normalize) * vec(ROW_G_NORM_W) * _silu(g_z)
    merged = merged + merge_term(3, y_d)

    y = x + _dot(merged.astype(BF16), wo_ref[...])
    if final_norm:
        y = _rms_normalize(y) * fg_ref[...]
    o_ref[...] = y


def _block_diag_tiles(w):
    n, b, _ = w.shape
    rows = w.astype(BF16).reshape(n * b // MXU_DIM, MXU_DIM, b)
    col = np.arange(MXU_DIM)
    repeat = (col[None, :] % b == np.arange(b)[:, None]).astype(np.float32)
    repeated = jnp.einsum('trj,jc->trc', rows, jnp.asarray(repeat, BF16), preferred_element_type=F32)
    own_block = col[:, None] // b == col[None, :] // b
    return jnp.where(own_block, repeated, 0.0).astype(BF16)


def _pad_key_heads(w):
    lead = w.shape[:-1]
    w = w.reshape(lead + (N_HEADS, GLA_HEAD_DK))
    w = jnp.pad(w, [(0, 0)] * len(lead) + [(0, 0), (0, HEAD_DIM - GLA_HEAD_DK)])
    return w.reshape(lead + (N_HEADS * HEAD_DIM,))


def _level_table(t):
    idx = np.arange(t)
    xor = idx[:, None] ^ idx[None, :]
    lv = np.floor(np.log2(np.maximum(xor, 1))).astype(np.int32) + 1
    lv = np.where(idx[:, None] == idx[None, :], 0, lv)
    return np.where(idx[:, None] >= idx[None, :], lv, -1).astype(np.int32)


def _pack_layer(l, w_in, lru_conv_w, lru_conv_b, lru_wa, lru_ba, lru_wx, lru_bx, lru_lambda,
                m_conv_w, m_conv_b, m_wq, m_wk, m_wv, m_wi, m_bi, m_wf, m_bf, m_norm_w, m_skip,
                h_norm_w, g_w_lr2, g_b_lr2, g_norm_w, w_branch, w_out):
    offs = np.concatenate([[0], np.cumsum(SPLIT_SIZES)])
    cols = [w_in[l][:, offs[i]:offs[i + 1]] for i in range(len(SPLIT_SIZES))]
    (lru_x, lru_z, m_x, m_o, m_z, h_q, h_f, h_i, h_z, g_q, g_k, g_v, g_lr, g_z, merge) = cols
    g_lr = jnp.pad(g_lr, ((0, 0), (0, LANES - GLA_RANK)))
    w_all = jnp.concatenate([lru_x, lru_z, m_x, m_o, m_z, h_q, h_f, h_i, h_z,
                             _pad_key_heads(g_q), _pad_key_heads(g_k), g_v, g_lr, g_z, merge], axis=1).astype(BF16)
    lru_w = jnp.concatenate([_block_diag_tiles(lru_wa[l]), _block_diag_tiles(lru_wx[l])], axis=2)
    m_qk = jnp.concatenate([_block_diag_tiles(m_wq[l]), _block_diag_tiles(m_wk[l])], axis=2)
    m_v = _block_diag_tiles(m_wv[l])
    w_gate = jnp.zeros((3 * W_BRANCH, 2 * LANES), F32)
    w_gate = w_gate.at[:, :N_HEADS].set(m_wi[l]).at[:, LANES:LANES + N_HEADS].set(m_wf[l])
    b_gate = jnp.zeros((1, 2 * LANES), F32)
    b_gate = b_gate.at[0, :N_HEADS].set(m_bi[l]).at[0, LANES:LANES + N_HEADS].set(m_bf[l])
    w_lr2 = jnp.pad(_pad_key_heads(g_w_lr2[l]), ((0, LANES - GLA_RANK), (0, 0)))
    rows = [lru_conv_w[l][j] for j in range(CONV_WIDTH)]
    rows += [lru_conv_b[l], lru_ba[l], lru_bx[l], lru_lambda[l]]
    rows += [m_conv_w[l][j] for j in range(CONV_WIDTH)]
    rows += [m_conv_b[l], m_norm_w[l], m_skip[l], jnp.tile(h_norm_w[l], N_HEADS), jnp.tile(g_norm_w[l], N_HEADS),
             _pad_key_heads(g_b_lr2[l])]
    vecs = jnp.stack(rows).astype(F32)
    vecs = jnp.pad(vecs, ((0, N_VEC_ROWS - vecs.shape[0]), (0, 0)))
    return (w_all, w_branch[l].astype(BF16), w_out[l].astype(BF16), lru_w, m_qk, m_v, w_gate.astype(BF16), b_gate,
            w_lr2.astype(BF16), vecs)


def _layer_call(layer, final_norm, batch, seq):
    n_tiles = seq // TILE
    tile_spec = pl.BlockSpec((TILE, D_MODEL), lambda b, j: (b * n_tiles + j, 0))
    resident = pl.BlockSpec(memory_space=pltpu.VMEM)
    state = functools.partial(pltpu.VMEM, dtype=F32)
    return pl.pallas_call(
        functools.partial(_layer_kernel, layer, final_norm),
        grid=(batch, n_tiles),
        in_specs=[tile_spec] + [resident] * 14,
        out_specs=tile_spec,
        out_shape=jax.ShapeDtypeStruct((batch * seq, D_MODEL), F32),
        scratch_shapes=[
            state((SUBLANES, W_BRANCH)),
            state((SUBLANES, W_BRANCH)),
            state((SUBLANES, W_BRANCH)),
            state((N_HEADS, HEAD_DIM, HEAD_DIM)),
            state((SUBLANES, W_BRANCH)),
            state((SUBLANES, LANES)),
            state((N_HEADS, HEAD_DIM, HEAD_DIM)),
            state((N_HEADS, HEAD_DIM, HEAD_DIM)),
        ],
        compiler_params=pltpu.CompilerParams(
            dimension_semantics=("arbitrary", "arbitrary"),
            vmem_limit_bytes=VMEM_LIMIT_BYTES,
        ),
        name=f"hybrid_layer_{layer}",
    )


def kernel(x, norm_g, w_in, lru_conv_w, lru_conv_b, lru_wa, lru_ba, lru_wx, lru_bx, lru_lambda, m_conv_w, m_conv_b, m_wq, m_wk, m_wv, m_wi, m_bi, m_wf, m_bf, m_norm_w, m_skip, h_lb_logits, h_norm_w, g_w_lr2, g_b_lr2, g_norm_w, w_branch, w_out, final_g):
    batch, seq, d = x.shape
    depth = w_in.shape[0]
    assert d == D_MODEL and seq % TILE == 0
    lvl = jnp.asarray(_level_table(TILE))
    lb_logits = h_lb_logits.astype(F32)
    y = x.reshape(batch * seq, d).astype(F32)
    for l in range(depth):
        packed = _pack_layer(l, w_in, lru_conv_w, lru_conv_b, lru_wa, lru_ba, lru_wx, lru_bx, lru_lambda,
                             m_conv_w, m_conv_b, m_wq, m_wk, m_wv, m_wi, m_bi, m_wf, m_bf, m_norm_w, m_skip,
                             h_norm_w, g_w_lr2, g_b_lr2, g_norm_w, w_branch, w_out)
        (w_all, w_br, w_o, lru_w, m_qk, m_v, w_gate, b_gate, w_lr2, vecs) = packed
        y = _layer_call(l, l == depth - 1, batch, seq)(
            y, norm_g[l].reshape(1, d), final_g.reshape(1, d), w_all, w_br, w_o, lru_w, m_qk, m_v, w_gate, b_gate,
            w_lr2, vecs, lb_logits, lvl)
    return y.reshape(batch, seq, d)
```

```python
import collections
import functools

import numpy as np
import jax
import jax.numpy as jnp
from jax import lax
from jax.experimental import pallas as pl
from jax.experimental.pallas import tpu as pltpu

D_MODEL = 1024
W_BRANCH = 512
N_BRANCH = 4
CONV_WIDTH = 4
LRU_BLOCKS = 8
LRU_C = 8.0
N_HEADS = 4
HEAD_DIM = W_BRANCH // N_HEADS
QKV_BLOCK = 4
GLA_DK = 256
GLA_HEAD_DK = GLA_DK // N_HEADS
GLA_RANK = 16
GLA_TAU = 16.0
NORM_EPS = 1e-6
SPLIT_SIZES = (512, 512, 512, 512, 512, 512, 512, 512, 512, 256, 256, 512, 16, 512, 4096)

LANES = 128
SUBLANES = 8
MXU_DIM = 256
VMEM_LIMIT_BYTES = 56 * 1024 * 1024

TILE = 256
PROJ_CHUNK = MXU_DIM

COL_A = 0
COL_B = COL_A + 2 * W_BRANCH
COL_C = COL_B + 3 * W_BRANCH
COL_D = COL_C + 4 * W_BRANCH
COL_M = COL_D + 2 * GLA_DK + 2 * W_BRANCH + LANES
COL_END = COL_M + N_BRANCH * D_MODEL

ROW_LRU_CONV_W = 0
ROW_LRU_CONV_B = 4
ROW_LRU_BA = 5
ROW_LRU_BX = 6
ROW_LRU_LAMBDA = 7
ROW_M_CONV_W = 8
ROW_M_CONV_B = 12
ROW_M_NORM_W = 13
ROW_M_SKIP = 14
ROW_H_NORM_W = 15
ROW_G_NORM_W = 16
ROW_G_B_LR2 = 17
N_VEC_ROWS = 24

BF16 = jnp.bfloat16
F32 = jnp.float32


class _SideWork:
    def __init__(self):
        self._items = collections.deque()

    def add(self, thunk):
        self._items.append(thunk)

    def step(self, n=1):
        for _ in range(n):
            if self._items:
                self._items.popleft()()

    def run_until(self, pieces, count):
        while len(pieces) < count:
            self._items.popleft()()

    def drain(self):
        while self._items:
            self._items.popleft()()


def _dot(a, b):
    return jnp.dot(a, b, preferred_element_type=F32)


def _dot_nt(a, b):
    return lax.dot_general(a, b, (((1,), (1,)), ((), ())), preferred_element_type=F32)


def _dot_tn(a, b):
    return lax.dot_general(a, b, (((0,), (0,)), ((), ())), preferred_element_type=F32)


def _sigmoid(x):
    return jax.nn.sigmoid(x)


def _silu(x):
    return x * _sigmoid(x)


def _log_sigmoid(x):
    return jnp.minimum(x, 0.0) - jnp.log(1.0 + jnp.exp(-jnp.abs(x)))


def _softplus(x):
    return jnp.maximum(x, 0.0) + jnp.log(1.0 + jnp.exp(-jnp.abs(x)))


def _to_tiles(x):
    t, c = x.shape
    return x.reshape(t // SUBLANES, SUBLANES, c)


def _from_tiles(x3):
    nb, s, c = x3.shape
    return x3.reshape(nb * s, c)


def _sub_iota(shape):
    return lax.broadcasted_iota(jnp.int32, shape, 1)


def _block_ref(x, b):
    t, c = x.shape
    if b == 1:
        x3 = _to_tiles(x)
        return _from_tiles(jnp.where((_sub_iota(x3.shape) & 1) == 1, pltpu.roll(x3, 1, 1), x3))
    if b == 2:
        x3 = _to_tiles(x)
        m = _sub_iota(x3.shape) & 3
        return _from_tiles(jnp.where(m == 0, pltpu.roll(x3, SUBLANES - 1, 1),
                                     jnp.where(m == 1, x3,
                                               jnp.where(m == 2, pltpu.roll(x3, 1, 1), pltpu.roll(x3, 2, 1)))))
    nb = t // (2 * b)
    x3 = x.reshape(nb, 2 * b, c)
    return jnp.broadcast_to(x3[:, b - 1:b, :], (nb, 2 * b, c)).reshape(t, c)


def _prefix_scan_rows(x, combine, identity):
    x3 = _to_tiles(x)
    nb, _, c = x3.shape
    j = _sub_iota(x3.shape)
    for k in (1, 2, 4):
        x3 = combine(jnp.where(j >= k, pltpu.roll(x3, k, 1), identity), x3)
    totals = x3[:, SUBLANES - 1:, :]
    carry = jnp.full((1, 1, c), identity, x.dtype)
    carries = []
    for i in range(nb):
        carries.append(carry)
        carry = combine(carry, totals[i:i + 1])
    return _from_tiles(combine(jnp.concatenate(carries, axis=0), x3))


def _linear_recurrence_rows(a, u, h_prev, side):
    a3, u3 = _to_tiles(a), _to_tiles(u)
    nb, _, c = a3.shape
    j = _sub_iota(a3.shape)
    for k in (1, 2, 4):
        a_back = jnp.where(j >= k, pltpu.roll(a3, k, 1), 1.0)
        u_back = jnp.where(j >= k, pltpu.roll(u3, k, 1), 0.0)
        u3 = a3 * u_back + u3
        a3 = a3 * a_back
        side.step(2)
    a_tot, u_tot = a3[:, SUBLANES - 1:, :], u3[:, SUBLANES - 1:, :]
    h_in = h_prev.reshape(1, 1, c)
    carries = []
    for i in range(nb):
        carries.append(h_in)
        h_in = u_tot[i:i + 1] + a_tot[i:i + 1] * h_in
    return _from_tiles(u3 + a3 * jnp.concatenate(carries, axis=0))


def _causal_conv(x, tail_ref, w, bias):
    t, c = x.shape
    x3 = _to_tiles(x)
    prev3 = jnp.concatenate([tail_ref[...].reshape(1, SUBLANES, c), x3[:-1]], axis=0)
    j = _sub_iota(x3.shape)
    y = bias + w[CONV_WIDTH - 1] * x
    for back in range(1, CONV_WIDTH):
        shifted = pltpu.roll(jnp.where(j >= SUBLANES - back, prev3, x3), back, 1)
        y = y + w[CONV_WIDTH - 1 - back] * _from_tiles(shifted)
    tail_ref[...] = x[t - SUBLANES:, :]
    return y


def _head_slices():
    return [slice(h * HEAD_DIM, (h + 1) * HEAD_DIM) for h in range(N_HEADS)]


def _gla_chunk(q, k, v, g, st_ref, lvl, head_dk, side):
    t = q.shape[0]
    per_tile = LANES // head_dk
    lane = lax.broadcasted_iota(jnp.int32, (1, LANES), 1)

    def key_tile(x, h):
        tile = h // per_tile
        return x[:, tile * LANES:(tile + 1) * LANES]

    def query_tile(x, h):
        if per_tile == 1:
            return key_tile(x, h)
        own = ((lane // head_dk) == (h % per_tile)).astype(BF16)
        return key_tile(x, h) * own

    q = q * (head_dk ** -0.5)
    vb = v.astype(BF16)
    qb = q.astype(BF16)
    kb = k.astype(BF16)
    heads = range(N_HEADS)
    zero = jnp.zeros((), BF16)
    scores = [jnp.where(lvl == 0, _dot_nt(query_tile(qb, h), key_tile(kb, h)).astype(BF16), zero) for h in heads]
    g_cum = _prefix_scan_rows(g, jnp.add, 0.0)
    b, level = 1, 1
    while b < t:
        eb = jnp.exp(-jnp.abs(g_cum - _block_ref(g_cum, b))).astype(BF16)
        qe = qb * eb
        ke = kb * eb
        scores = [jnp.where(lvl == level, _dot_nt(query_tile(qe, h), key_tile(ke, h)).astype(BF16), s)
                  for h, s in zip(heads, scores)]
        side.step()
        b *= 2
        level += 1
    g_last = g_cum[t - 1:t, :]
    q_in = (q * jnp.exp(g_cum)).astype(BF16)
    k_out = (k * jnp.exp(g_last - g_cum)).astype(BF16)
    state_decay = jnp.exp(g_last)
    outs = []
    for h in heads:
        sl = slice(h * HEAD_DIM, (h + 1) * HEAD_DIM)
        st = st_ref[h]
        outs.append(_dot(scores[h], vb[:, sl]) + _dot_nt(query_tile(q_in, h), st.astype(BF16)))
        st_ref[h] = st * key_tile(state_decay, h) + _dot_tn(vb[:, sl], key_tile(k_out, h))
        side.step()
    return jnp.concatenate(outs, axis=1)


def _mlstm_chunk(q, k, v, log_i, log_f, c_ref, n_ref, m_ref, causal_bias, side):
    t = q.shape[0]
    heads = _head_slices()
    b = _prefix_scan_rows(log_f, jnp.add, 0.0)
    z = log_i - b
    zmax = _prefix_scan_rows(z, jnp.maximum, -jnp.inf)
    m_prev = m_ref[0:1, :]
    m_t = b + jnp.maximum(m_prev, zmax)
    col = b - m_t
    w_inter = jnp.exp(b + m_prev - m_t)
    inv_floor = jnp.exp(-m_t)
    z_rows = z.T
    b_last = b[t - 1:t, :]
    m_new = m_t[t - 1:t, :]
    w_state = jnp.exp(z + b_last - m_new)
    decay = jnp.exp(b_last + m_prev - m_new)
    qs = q * (HEAD_DIM ** -0.5)
    qb = qs.astype(BF16)
    kb = k.astype(BF16)
    vb = v.astype(BF16)
    n_prev = n_ref[0:1, :]
    outs, n_new = [], []
    for h, sl in enumerate(heads):
        hl = slice(h, h + 1)
        dm = col[:, hl] + z_rows[hl, :] + causal_bias
        s = _dot_nt(qb[:, sl], kb[:, sl]) * jnp.exp(dm)
        c_h = c_ref[h]
        wi = w_inter[:, hl]
        num = _dot(s.astype(BF16), vb[:, sl]) + wi * _dot(qb[:, sl], c_h.astype(BF16))
        den = (jnp.sum(s, axis=-1, keepdims=True)
               + wi * jnp.sum(qs[:, sl] * n_prev[:, sl], axis=-1, keepdims=True))
        outs.append(num / jnp.maximum(jnp.abs(den), inv_floor[:, hl]))
        kw = k[:, sl] * w_state[:, hl]
        d_h = decay[:, hl]
        c_ref[h] = d_h * c_h + _dot_tn(kw.astype(BF16), vb[:, sl])
        n_new.append(d_h * n_prev[:, sl] + jnp.sum(kw, axis=0, keepdims=True))
        side.step(2)
    n_ref[0:1, :] = jnp.concatenate(n_new, axis=1)
    m_ref[0:1, :] = m_new
    return jnp.concatenate(outs, axis=1)


def _per_head(x, fn):
    return jnp.concatenate([fn(x[:, sl]) for sl in _head_slices()], axis=1)


def _rms_normalize(x):
    return x * lax.rsqrt(jnp.mean(x * x, axis=-1, keepdims=True) + NORM_EPS)


def _layer_normalize(x):
    xc = x - jnp.mean(x, axis=-1, keepdims=True)
    return xc * lax.rsqrt(jnp.mean(xc * xc, axis=-1, keepdims=True) + NORM_EPS)


def _layer_kernel(layer, final_norm,
                  x_ref, ng_ref, fg_ref, w_ref, wb_ref, wo_ref, lruw_ref, mqk_ref, mv_ref, wg_ref, bg_ref,
                  w2_ref, vec_ref, lbl_ref, lvl_ref, cb_ref,
                  o_ref,
                  conv_a, conv_b, lru_h, m_c, m_n, m_m, h_s, g_s):
    half = MXU_DIM

    @pl.when(pl.program_id(1) == 0)
    def _reset_state():
        conv_a[...] = jnp.zeros(conv_a.shape, F32)
        conv_b[...] = jnp.zeros(conv_b.shape, F32)
        lru_h[...] = jnp.zeros(lru_h.shape, F32)
        m_c[...] = jnp.zeros(m_c.shape, F32)
        m_n[...] = jnp.zeros(m_n.shape, F32)
        m_m[...] = jnp.zeros(m_m.shape, F32)
        h_s[...] = jnp.zeros(h_s.shape, F32)
        g_s[...] = jnp.zeros(g_s.shape, F32)

    def vec(row):
        return vec_ref[row:row + 1, :]

    lvl = lvl_ref[...]
    x = x_ref[...]
    hb = (_rms_normalize(x) * ng_ref[...]).astype(BF16)
    side = _SideWork()

    def cat(pieces):
        return jnp.concatenate(pieces, axis=1)

    def queue_in_proj(lo, hi):
        pieces = []
        for c in range(lo, hi, PROJ_CHUNK):
            side.add(lambda c=c: pieces.append(_dot(hb, w_ref[:, c:min(c + PROJ_CHUNK, hi)])))
        return pieces

    def take_in_proj(pieces, lo, hi):
        side.run_until(pieces, -(-(hi - lo) // PROJ_CHUNK))
        return cat(pieces)

    def queue_merge_gate(branch):
        pieces = []
        base = COL_M + branch * D_MODEL
        for c in range(base, base + D_MODEL, PROJ_CHUNK):
            side.add(lambda c=c: pieces.append(_sigmoid(_dot(hb, w_ref[:, c:c + PROJ_CHUNK]))))
        return pieces

    def queue_branch_proj(branch, y):
        out = []
        yb = y.astype(BF16)
        side.add(lambda: out.append(_dot(yb, wb_ref[branch])))
        return out

    u = _dot(hb, w_ref[:, COL_A:COL_B])
    u_b, u_c, u_d = queue_in_proj(COL_B, COL_C), queue_in_proj(COL_C, COL_D), queue_in_proj(COL_D, COL_M)
    gate = [queue_merge_gate(n) for n in range(N_BRANCH)]

    lru_x, lru_z = u[:, :W_BRANCH], u[:, W_BRANCH:]
    xa = _causal_conv(lru_x, conv_a, [vec(ROW_LRU_CONV_W + j) for j in range(CONV_WIDTH)], vec(ROW_LRU_CONV_B))
    xab = xa.astype(BF16)
    g0 = _dot(xab[:, :half], lruw_ref[0])
    g1 = _dot(xab[:, half:], lruw_ref[1])
    side.step(2)
    r = _sigmoid(jnp.concatenate([g0[:, :half], g1[:, :half]], axis=1) + vec(ROW_LRU_BA))
    i = _sigmoid(jnp.concatenate([g0[:, half:], g1[:, half:]], axis=1) + vec(ROW_LRU_BX))
    log_a = (-LRU_C) * r * _softplus(-vec(ROW_LRU_LAMBDA))
    a = jnp.exp(log_a)
    side.step(2)
    h_lru = _linear_recurrence_rows(a, jnp.sqrt(1.0 - a * a) * (i * xa),
                                    lru_h[SUBLANES - 1:SUBLANES, :], side)
    lru_h[...] = h_lru[TILE - SUBLANES:, :]
    y_a = h_lru * _silu(lru_z)

    u = take_in_proj(u_b, COL_B, COL_C)
    proj_0 = queue_branch_proj(0, y_a)
    m_x, m_o, m_z = u[:, :W_BRANCH], u[:, W_BRANCH:2 * W_BRANCH], u[:, 2 * W_BRANCH:]
    xm = _silu(_causal_conv(m_x, conv_b, [vec(ROW_M_CONV_W + j) for j in range(CONV_WIDTH)], vec(ROW_M_CONV_B)))
    xmb = xm.astype(BF16)
    mxb = m_x.astype(BF16)
    qk0 = _dot(xmb[:, :half], mqk_ref[0])
    qk1 = _dot(xmb[:, half:], mqk_ref[1])
    q = jnp.concatenate([qk0[:, :half], qk1[:, :half]], axis=1)
    k = jnp.concatenate([qk0[:, half:], qk1[:, half:]], axis=1)
    v = jnp.concatenate([_dot(mxb[:, :half], mv_ref[0]), _dot(mxb[:, half:], mv_ref[1])], axis=1)
    side.step()
    gates = _dot(jnp.concatenate([q, k, v], axis=1).astype(BF16), wg_ref[...]) + bg_ref[...]
    hm = _mlstm_chunk(q, k, v, gates[:, :LANES], _log_sigmoid(gates[:, LANES:]), m_c, m_n, m_m, cb_ref[...], side)
    hm = _sigmoid(m_o) * hm
    y_b = (_per_head(hm, _layer_normalize) * vec(ROW_M_NORM_W) + vec(ROW_M_SKIP) * xm) * _silu(m_z)

    u = take_in_proj(u_c, COL_C, COL_D)
    proj_1 = queue_branch_proj(1, y_b)
    h_q, h_f = u[:, :W_BRANCH], u[:, W_BRANCH:2 * W_BRANCH]
    h_i, h_z = u[:, 2 * W_BRANCH:3 * W_BRANCH], u[:, 3 * W_BRANCH:]
    logits = lbl_ref[...]
    ex = jnp.exp(logits - jnp.max(logits, axis=0, keepdims=True))
    sm = ex / jnp.sum(ex, axis=0, keepdims=True)
    lb = jnp.zeros((1, W_BRANCH), F32)
    for j in range(1, layer + 1):
        lb = lb + sm[j:j + 1, :]
    e_f = jnp.exp(-jnp.abs(h_f))
    r_f = 1.0 / (1.0 + e_f)
    sig_neg = jnp.where(h_f >= 0, e_f * r_f, r_f)
    lo = jnp.log(lb)
    hi = jnp.log1p(-lb) + (jnp.minimum(h_f, 0.0) - jnp.log(1.0 + e_f))
    log_fh = jnp.maximum(lo, hi) + jnp.log(1.0 + jnp.exp(-jnp.abs(lo - hi)))
    side.step()
    oh = _gla_chunk(_silu(h_q), (1.0 - lb) * sig_neg, h_i, log_fh, h_s, lvl, HEAD_DIM, side)
    y_c = _per_head(oh, _rms_normalize) * vec(ROW_H_NORM_W) * _silu(h_z)

    u = take_in_proj(u_d, COL_D, COL_M)
    proj_2 = queue_branch_proj(2, y_c)
    g_q, g_k, g_v = u[:, :GLA_DK], u[:, GLA_DK:2 * GLA_DK], u[:, 2 * GLA_DK:2 * GLA_DK + W_BRANCH]
    g_lr = u[:, 2 * GLA_DK + W_BRANCH:2 * GLA_DK + W_BRANCH + LANES]
    g_z = u[:, 2 * GLA_DK + W_BRANCH + LANES:]
    log_gk = (_log_sigmoid(_dot(g_lr.astype(BF16), w2_ref[...]) + vec(ROW_G_B_LR2)[:, :GLA_DK])
              * (1.0 / GLA_TAU))
    og = _gla_chunk(g_q, g_k, g_v, log_gk, g_s, lvl, GLA_HEAD_DK, side)
    y_d = _per_head(og, _rms_normalize) * vec(ROW_G_NORM_W) * _silu(g_z)

    side.drain()
    merged = cat(gate[0]) * proj_0[0] + cat(gate[1]) * proj_1[0] + cat(gate[2]) * proj_2[0]
    merged = merged + cat(gate[3]) * _dot(y_d.astype(BF16), wb_ref[3])
    y = x + _dot(merged.astype(BF16), wo_ref[...])
    if final_norm:
        y = _rms_normalize(y) * fg_ref[...]
    o_ref[...] = y


def _block_diag_tiles(w):
    n, b, _ = w.shape
    rows = w.astype(BF16).reshape(n * b // MXU_DIM, MXU_DIM, b)
    col = np.arange(MXU_DIM)
    repeat = (col[None, :] % b == np.arange(b)[:, None]).astype(np.float32)
    repeated = jnp.einsum('trj,jc->trc', rows, jnp.asarray(repeat, BF16), preferred_element_type=F32)
    own_block = col[:, None] // b == col[None, :] // b
    return jnp.where(own_block, repeated, 0.0).astype(BF16)


def _level_table(t):
    idx = np.arange(t)
    xor = idx[:, None] ^ idx[None, :]
    lv = np.floor(np.log2(np.maximum(xor, 1))).astype(np.int32) + 1
    lv = np.where(idx[:, None] == idx[None, :], 0, lv)
    return np.where(idx[:, None] >= idx[None, :], lv, -1).astype(np.float32)


def _causal_bias(t):
    idx = np.arange(t)
    return np.where(idx[:, None] >= idx[None, :], 0.0, -np.inf).astype(np.float32)


def _pack_layer(l, w_in, lru_conv_w, lru_conv_b, lru_wa, lru_ba, lru_wx, lru_bx, lru_lambda,
                m_conv_w, m_conv_b, m_wq, m_wk, m_wv, m_wi, m_bi, m_wf, m_bf, m_norm_w, m_skip,
                h_norm_w, g_w_lr2, g_b_lr2, g_norm_w, w_branch, w_out):
    offs = np.concatenate([[0], np.cumsum(SPLIT_SIZES)])
    cols = [w_in[l][:, offs[i]:offs[i + 1]] for i in range(len(SPLIT_SIZES))]
    (lru_x, lru_z, m_x, m_o, m_z, h_q, h_f, h_i, h_z, g_q, g_k, g_v, g_lr, g_z, merge) = cols
    g_lr = jnp.pad(g_lr, ((0, 0), (0, LANES - GLA_RANK)))
    w_all = jnp.concatenate([lru_x, lru_z, m_x, m_o, m_z, h_q, h_f, h_i, h_z,
                             g_q, g_k, g_v, g_lr, g_z, merge], axis=1).astype(BF16)
    lru_w = jnp.concatenate([_block_diag_tiles(lru_wa[l]), _block_diag_tiles(lru_wx[l])], axis=2)
    m_qk = jnp.concatenate([_block_diag_tiles(m_wq[l]), _block_diag_tiles(m_wk[l])], axis=2)
    m_v = _block_diag_tiles(m_wv[l])
    w_gate = jnp.zeros((3 * W_BRANCH, 2 * LANES), F32)
    w_gate = w_gate.at[:, :N_HEADS].set(m_wi[l]).at[:, LANES:LANES + N_HEADS].set(m_wf[l])
    b_gate = jnp.zeros((1, 2 * LANES), F32)
    b_gate = b_gate.at[0, :N_HEADS].set(m_bi[l]).at[0, LANES:LANES + N_HEADS].set(m_bf[l])
    w_lr2 = jnp.pad(g_w_lr2[l], ((0, LANES - GLA_RANK), (0, 0)))
    rows = [lru_conv_w[l][j] for j in range(CONV_WIDTH)]
    rows += [lru_conv_b[l], lru_ba[l], lru_bx[l], lru_lambda[l]]
    rows += [m_conv_w[l][j] for j in range(CONV_WIDTH)]
    rows += [m_conv_b[l], m_norm_w[l], m_skip[l], jnp.tile(h_norm_w[l], N_HEADS), jnp.tile(g_norm_w[l], N_HEADS),
             jnp.pad(g_b_lr2[l], (0, W_BRANCH - GLA_DK))]
    vecs = jnp.stack(rows).astype(F32)
    vecs = jnp.pad(vecs, ((0, N_VEC_ROWS - vecs.shape[0]), (0, 0)))
    return (w_all, w_branch[l].astype(BF16), w_out[l].astype(BF16), lru_w, m_qk, m_v, w_gate.astype(BF16), b_gate,
            w_lr2.astype(BF16), vecs)


def _layer_call(layer, final_norm, batch, seq):
    n_tiles = seq // TILE
    tile_spec = pl.BlockSpec((TILE, D_MODEL), lambda b, j: (b * n_tiles + j, 0))
    resident = pl.BlockSpec(memory_space=pltpu.VMEM)
    state = functools.partial(pltpu.VMEM, dtype=F32)
    return pl.pallas_call(
        functools.partial(_layer_kernel, layer, final_norm),
        grid=(batch, n_tiles),
        in_specs=[tile_spec] + [resident] * 15,
        out_specs=tile_spec,
        out_shape=jax.ShapeDtypeStruct((batch * seq, D_MODEL), F32),
        scratch_shapes=[
            state((SUBLANES, W_BRANCH)),
            state((SUBLANES, W_BRANCH)),
            state((SUBLANES, W_BRANCH)),
            state((N_HEADS, HEAD_DIM, HEAD_DIM)),
            state((SUBLANES, W_BRANCH)),
            state((SUBLANES, LANES)),
            state((N_HEADS, HEAD_DIM, HEAD_DIM)),
            state((N_HEADS, HEAD_DIM, LANES)),
        ],
        compiler_params=pltpu.CompilerParams(
            dimension_semantics=("arbitrary", "arbitrary"),
            vmem_limit_bytes=VMEM_LIMIT_BYTES,
        ),
        name=f"hybrid_layer_{layer}",
    )


def kernel(x, norm_g, w_in, lru_conv_w, lru_conv_b, lru_wa, lru_ba, lru_wx, lru_bx, lru_lambda, m_conv_w, m_conv_b, m_wq, m_wk, m_wv, m_wi, m_bi, m_wf, m_bf, m_norm_w, m_skip, h_lb_logits, h_norm_w, g_w_lr2, g_b_lr2, g_norm_w, w_branch, w_out, final_g):
    batch, seq, d = x.shape
    depth = w_in.shape[0]
    assert d == D_MODEL and seq % TILE == 0
    lvl = jnp.asarray(_level_table(TILE), BF16)
    causal_bias = jnp.asarray(_causal_bias(TILE))
    lb_logits = h_lb_logits.astype(F32)
    y = x.reshape(batch * seq, d).astype(F32)
    for l in range(depth):
        packed = _pack_layer(l, w_in, lru_conv_w, lru_conv_b, lru_wa, lru_ba, lru_wx, lru_bx, lru_lambda,
                             m_conv_w, m_conv_b, m_wq, m_wk, m_wv, m_wi, m_bi, m_wf, m_bf, m_norm_w, m_skip,
                             h_norm_w, g_w_lr2, g_b_lr2, g_norm_w, w_branch, w_out)
        (w_all, w_br, w_o, lru_w, m_qk, m_v, w_gate, b_gate, w_lr2, vecs) = packed
        y = _layer_call(l, l == depth - 1, batch, seq)(
            y, norm_g[l].reshape(1, d), final_g.reshape(1, d), w_all, w_br, w_o, lru_w, m_qk, m_v, w_gate, b_gate,
            w_lr2, vecs, lb_logits, lvl, causal_bias)
    return y.reshape(batch, seq, d)
```

```python
import collections
import functools

import numpy as np
import jax
import jax.numpy as jnp
from jax import lax
from jax.experimental import pallas as pl
from jax.experimental.pallas import tpu as pltpu

D_MODEL = 1024
W_BRANCH = 512
N_BRANCH = 4
CONV_WIDTH = 4
LRU_BLOCKS = 8
LRU_C = 8.0
N_HEADS = 4
HEAD_DIM = W_BRANCH // N_HEADS
QKV_BLOCK = 4
GLA_DK = 256
GLA_HEAD_DK = GLA_DK // N_HEADS
GLA_RANK = 16
GLA_TAU = 16.0
NORM_EPS = 1e-6
SPLIT_SIZES = (512, 512, 512, 512, 512, 512, 512, 512, 512, 256, 256, 512, 16, 512, 4096)

LANES = 128
SUBLANES = 8
MXU_DIM = 256
VMEM_LIMIT_BYTES = 56 * 1024 * 1024

TILE = 256
PROJ_CHUNK = MXU_DIM

COL_A = 0
COL_B = COL_A + 2 * W_BRANCH
COL_C = COL_B + 3 * W_BRANCH
COL_D = COL_C + 4 * W_BRANCH
COL_LR = COL_D + 2 * GLA_DK + W_BRANCH
COL_M = COL_LR + LANES + W_BRANCH
COL_END = COL_M + N_BRANCH * D_MODEL
PACK_ROWS = 128

ROW_LRU_CONV_W = 0
ROW_LRU_CONV_B = 4
ROW_LRU_BA = 5
ROW_LRU_BX = 6
ROW_LRU_LAMBDA = 7
ROW_M_CONV_W = 8
ROW_M_CONV_B = 12
ROW_M_NORM_W = 13
ROW_M_SKIP = 14
ROW_H_NORM_W = 15
ROW_G_NORM_W = 16
ROW_G_B_LR2 = 17
N_VEC_ROWS = 24

BF16 = jnp.bfloat16
F32 = jnp.float32


class _SideWork:
    def __init__(self):
        self._items = collections.deque()

    def add(self, thunk):
        self._items.append(thunk)

    def step(self, n=1):
        for _ in range(n):
            if self._items:
                self._items.popleft()()

    def run_until(self, pieces, count):
        while len(pieces) < count:
            self._items.popleft()()

    def drain(self):
        while self._items:
            self._items.popleft()()


def _dot(a, b):
    return jnp.dot(a, b, preferred_element_type=F32)


def _dot_nt(a, b):
    return lax.dot_general(a, b, (((1,), (1,)), ((), ())), preferred_element_type=F32)


def _dot_tn(a, b):
    return lax.dot_general(a, b, (((0,), (0,)), ((), ())), preferred_element_type=F32)


def _sigmoid(x):
    return jax.nn.sigmoid(x)


def _silu(x):
    return x * _sigmoid(x)


def _log_sigmoid(x):
    return jnp.minimum(x, 0.0) - jnp.log(1.0 + jnp.exp(-jnp.abs(x)))


def _softplus(x):
    return jnp.maximum(x, 0.0) + jnp.log(1.0 + jnp.exp(-jnp.abs(x)))


def _to_tiles(x):
    t, c = x.shape
    return x.reshape(t // SUBLANES, SUBLANES, c)


def _from_tiles(x3):
    nb, s, c = x3.shape
    return x3.reshape(nb * s, c)


def _sub_iota(shape):
    return lax.broadcasted_iota(jnp.int32, shape, 1)


def _block_ref(x, b):
    t, c = x.shape
    if b == 1:
        x3 = _to_tiles(x)
        return _from_tiles(jnp.where((_sub_iota(x3.shape) & 1) == 1, pltpu.roll(x3, 1, 1), x3))
    if b == 2:
        x3 = _to_tiles(x)
        m = _sub_iota(x3.shape) & 3
        return _from_tiles(jnp.where(m == 0, pltpu.roll(x3, SUBLANES - 1, 1),
                                     jnp.where(m == 1, x3,
                                               jnp.where(m == 2, pltpu.roll(x3, 1, 1), pltpu.roll(x3, 2, 1)))))
    nb = t // (2 * b)
    x3 = x.reshape(nb, 2 * b, c)
    return jnp.broadcast_to(x3[:, b - 1:b, :], (nb, 2 * b, c)).reshape(t, c)


def _prefix_scan_rows(x, combine, identity):
    x3 = _to_tiles(x)
    nb, _, c = x3.shape
    j = _sub_iota(x3.shape)
    for k in (1, 2, 4):
        x3 = combine(jnp.where(j >= k, pltpu.roll(x3, k, 1), identity), x3)
    totals = x3[:, SUBLANES - 1:, :]
    carry = jnp.full((1, 1, c), identity, x.dtype)
    carries = []
    for i in range(nb):
        carries.append(carry)
        carry = combine(carry, totals[i:i + 1])
    return _from_tiles(combine(jnp.concatenate(carries, axis=0), x3))


def _linear_recurrence_rows(a, u, h_prev, side):
    a3, u3 = _to_tiles(a), _to_tiles(u)
    nb, _, c = a3.shape
    j = _sub_iota(a3.shape)
    for k in (1, 2, 4):
        a_back = jnp.where(j >= k, pltpu.roll(a3, k, 1), 1.0)
        u_back = jnp.where(j >= k, pltpu.roll(u3, k, 1), 0.0)
        u3 = a3 * u_back + u3
        a3 = a3 * a_back
        side.step(2)
    a_tot, u_tot = a3[:, SUBLANES - 1:, :], u3[:, SUBLANES - 1:, :]
    h_in = h_prev.reshape(1, 1, c)
    carries = []
    for i in range(nb):
        carries.append(h_in)
        h_in = u_tot[i:i + 1] + a_tot[i:i + 1] * h_in
    return _from_tiles(u3 + a3 * jnp.concatenate(carries, axis=0))


def _causal_conv(x, tail_ref, w, bias):
    t, c = x.shape
    x3 = _to_tiles(x)
    prev3 = jnp.concatenate([tail_ref[...].reshape(1, SUBLANES, c), x3[:-1]], axis=0)
    j = _sub_iota(x3.shape)
    y = bias + w[CONV_WIDTH - 1] * x
    for back in range(1, CONV_WIDTH):
        shifted = pltpu.roll(jnp.where(j >= SUBLANES - back, prev3, x3), back, 1)
        y = y + w[CONV_WIDTH - 1 - back] * _from_tiles(shifted)
    tail_ref[...] = x[t - SUBLANES:, :]
    return y


def _head_slices():
    return [slice(h * HEAD_DIM, (h + 1) * HEAD_DIM) for h in range(N_HEADS)]


def _gla_chunk(q, k, v, g, st_ref, lvl, head_dk, side):
    t = q.shape[0]
    per_tile = LANES // head_dk
    lane = lax.broadcasted_iota(jnp.int32, (1, LANES), 1)

    def key_tile(x, h):
        tile = h // per_tile
        return x[:, tile * LANES:(tile + 1) * LANES]

    def query_tile(x, h):
        if per_tile == 1:
            return key_tile(x, h)
        own = ((lane // head_dk) == (h % per_tile)).astype(BF16)
        return key_tile(x, h) * own

    q = q * (head_dk ** -0.5)
    vb = v.astype(BF16)
    qb = q.astype(BF16)
    kb = k.astype(BF16)
    heads = range(N_HEADS)
    zero = jnp.zeros((), BF16)
    scores = [jnp.where(lvl == 0, _dot_nt(query_tile(qb, h), key_tile(kb, h)).astype(BF16), zero) for h in heads]
    g_cum = _prefix_scan_rows(g, jnp.add, 0.0)
    b, level = 1, 1
    while b < t:
        eb = jnp.exp(-jnp.abs(g_cum - _block_ref(g_cum, b))).astype(BF16)
        qe = qb * eb
        ke = kb * eb
        scores = [jnp.where(lvl == level, _dot_nt(query_tile(qe, h), key_tile(ke, h)).astype(BF16), s)
                  for h, s in zip(heads, scores)]
        side.step()
        b *= 2
        level += 1
    g_last = g_cum[t - 1:t, :]
    q_in = (q * jnp.exp(g_cum)).astype(BF16)
    k_out = (k * jnp.exp(g_last - g_cum)).astype(BF16)
    state_decay = jnp.exp(g_last)
    outs = []
    for h in heads:
        sl = slice(h * HEAD_DIM, (h + 1) * HEAD_DIM)
        st = st_ref[h]
        outs.append(_dot(scores[h], vb[:, sl]) + _dot_nt(query_tile(q_in, h), st.astype(BF16)))
        st_ref[h] = st * key_tile(state_decay, h) + _dot_tn(vb[:, sl], key_tile(k_out, h))
        side.step()
    return jnp.concatenate(outs, axis=1)


def _mlstm_chunk(q, k, v, log_i, log_f, c_ref, n_ref, m_ref, causal_bias, side):
    t = q.shape[0]
    heads = _head_slices()
    b = _prefix_scan_rows(log_f, jnp.add, 0.0)
    z = log_i - b
    zmax = _prefix_scan_rows(z, jnp.maximum, -jnp.inf)
    m_prev = m_ref[0:1, :]
    m_t = b + jnp.maximum(m_prev, zmax)
    col = b - m_t
    w_inter = jnp.exp(b + m_prev - m_t)
    inv_floor = jnp.exp(-m_t)
    z_rows = z.T
    b_last = b[t - 1:t, :]
    m_new = m_t[t - 1:t, :]
    w_state = jnp.exp(z + b_last - m_new)
    decay = jnp.exp(b_last + m_prev - m_new)
    qs = q * (HEAD_DIM ** -0.5)
    qb = qs.astype(BF16)
    kb = k.astype(BF16)
    vb = v.astype(BF16)
    n_prev = n_ref[0:1, :]
    outs, n_new = [], []
    for h, sl in enumerate(heads):
        hl = slice(h, h + 1)
        dm = col[:, hl] + z_rows[hl, :] + causal_bias
        s = _dot_nt(qb[:, sl], kb[:, sl]) * jnp.exp(dm)
        c_h = c_ref[h]
        wi = w_inter[:, hl]
        num = _dot(s.astype(BF16), vb[:, sl]) + wi * _dot(qb[:, sl], c_h.astype(BF16))
        den = (jnp.sum(s, axis=-1, keepdims=True)
               + wi * jnp.sum(qs[:, sl] * n_prev[:, sl], axis=-1, keepdims=True))
        outs.append(num / jnp.maximum(jnp.abs(den), inv_floor[:, hl]))
        kw = k[:, sl] * w_state[:, hl]
        d_h = decay[:, hl]
        c_ref[h] = d_h * c_h + _dot_tn(kw.astype(BF16), vb[:, sl])
        n_new.append(d_h * n_prev[:, sl] + jnp.sum(kw, axis=0, keepdims=True))
        side.step(2)
    n_ref[0:1, :] = jnp.concatenate(n_new, axis=1)
    m_ref[0:1, :] = m_new
    return jnp.concatenate(outs, axis=1)


def _per_head(x, fn):
    return jnp.concatenate([fn(x[:, sl]) for sl in _head_slices()], axis=1)


def _rms_normalize(x):
    return x * lax.rsqrt(jnp.mean(x * x, axis=-1, keepdims=True) + NORM_EPS)


def _layer_normalize(x):
    xc = x - jnp.mean(x, axis=-1, keepdims=True)
    return xc * lax.rsqrt(jnp.mean(xc * xc, axis=-1, keepdims=True) + NORM_EPS)


def _layer_kernel(layer, final_norm,
                  x_ref, ng_ref, fg_ref, w_ref, wb_ref, wo_ref, lruw_ref, mqk_ref, mv_ref, wg_ref, bg_ref,
                  w2_ref, vec_ref, lbl_ref, lvl_ref, cb_ref,
                  o_ref,
                  conv_a, conv_b, lru_h, m_c, m_n, m_m, h_s, g_s):
    half = MXU_DIM

    @pl.when(pl.program_id(1) == 0)
    def _reset_state():
        conv_a[...] = jnp.zeros(conv_a.shape, F32)
        conv_b[...] = jnp.zeros(conv_b.shape, F32)
        lru_h[...] = jnp.zeros(lru_h.shape, F32)
        m_c[...] = jnp.zeros(m_c.shape, F32)
        m_n[...] = jnp.zeros(m_n.shape, F32)
        m_m[...] = jnp.zeros(m_m.shape, F32)
        h_s[...] = jnp.zeros(h_s.shape, F32)
        g_s[...] = jnp.zeros(g_s.shape, F32)

    def vec(row):
        return vec_ref[row:row + 1, :]

    lvl = lvl_ref[...]
    x = x_ref[...]
    hb = (_rms_normalize(x) * ng_ref[...]).astype(BF16)
    side = _SideWork()

    def cat(pieces):
        return jnp.concatenate(pieces, axis=1)

    def queue_in_proj(lo, hi):
        pieces = []
        for c in range(lo, hi, PROJ_CHUNK):
            side.add(lambda c=c: pieces.append(_dot(hb, w_ref[:, c:min(c + PROJ_CHUNK, hi)])))
        return pieces

    def take_in_proj(pieces, lo, hi):
        side.run_until(pieces, -(-(hi - lo) // PROJ_CHUNK))
        return cat(pieces)

    def queue_merge_gate(branch):
        pieces = []
        base = COL_M + branch * D_MODEL
        for c in range(base, base + D_MODEL, PROJ_CHUNK):
            side.add(lambda c=c: pieces.append(_sigmoid(_dot(hb, w_ref[:, c:c + PROJ_CHUNK]))))
        return pieces

    def queue_branch_proj(branch, y):
        out = []
        yb = y.astype(BF16)
        side.add(lambda: out.append(_dot(yb, wb_ref[branch])))
        return out

    u = _dot(hb, w_ref[:, COL_A:COL_B])
    u_b, u_c, u_d = queue_in_proj(COL_B, COL_C), queue_in_proj(COL_C, COL_D), queue_in_proj(COL_D, COL_M)
    gate = [queue_merge_gate(n) for n in range(N_BRANCH)]

    lru_x, lru_z = u[:, :W_BRANCH], u[:, W_BRANCH:]
    xa = _causal_conv(lru_x, conv_a, [vec(ROW_LRU_CONV_W + j) for j in range(CONV_WIDTH)], vec(ROW_LRU_CONV_B))
    xab = xa.astype(BF16)
    g0 = _dot(xab[:, :half], lruw_ref[0])
    g1 = _dot(xab[:, half:], lruw_ref[1])
    side.step(2)
    r = _sigmoid(jnp.concatenate([g0[:, :half], g1[:, :half]], axis=1) + vec(ROW_LRU_BA))
    i = _sigmoid(jnp.concatenate([g0[:, half:], g1[:, half:]], axis=1) + vec(ROW_LRU_BX))
    log_a = (-LRU_C) * r * _softplus(-vec(ROW_LRU_LAMBDA))
    a = jnp.exp(log_a)
    side.step(2)
    h_lru = _linear_recurrence_rows(a, jnp.sqrt(1.0 - a * a) * (i * xa),
                                    lru_h[SUBLANES - 1:SUBLANES, :], side)
    lru_h[...] = h_lru[TILE - SUBLANES:, :]
    y_a = h_lru * _silu(lru_z)

    u = take_in_proj(u_b, COL_B, COL_C)
    proj_0 = queue_branch_proj(0, y_a)
    m_x, m_o, m_z = u[:, :W_BRANCH], u[:, W_BRANCH:2 * W_BRANCH], u[:, 2 * W_BRANCH:]
    xm = _silu(_causal_conv(m_x, conv_b, [vec(ROW_M_CONV_W + j) for j in range(CONV_WIDTH)], vec(ROW_M_CONV_B)))
    xmb = xm.astype(BF16)
    mxb = m_x.astype(BF16)
    qk0 = _dot(xmb[:, :half], mqk_ref[0])
    qk1 = _dot(xmb[:, half:], mqk_ref[1])
    q = jnp.concatenate([qk0[:, :half], qk1[:, :half]], axis=1)
    k = jnp.concatenate([qk0[:, half:], qk1[:, half:]], axis=1)
    v = jnp.concatenate([_dot(mxb[:, :half], mv_ref[0]), _dot(mxb[:, half:], mv_ref[1])], axis=1)
    side.step()
    gates = _dot(jnp.concatenate([q, k, v], axis=1).astype(BF16), wg_ref[...]) + bg_ref[...]
    hm = _mlstm_chunk(q, k, v, gates[:, :LANES], _log_sigmoid(gates[:, LANES:]), m_c, m_n, m_m, cb_ref[...], side)
    hm = _sigmoid(m_o) * hm
    y_b = (_per_head(hm, _layer_normalize) * vec(ROW_M_NORM_W) + vec(ROW_M_SKIP) * xm) * _silu(m_z)

    u = take_in_proj(u_c, COL_C, COL_D)
    proj_1 = queue_branch_proj(1, y_b)
    h_q, h_f = u[:, :W_BRANCH], u[:, W_BRANCH:2 * W_BRANCH]
    h_i, h_z = u[:, 2 * W_BRANCH:3 * W_BRANCH], u[:, 3 * W_BRANCH:]
    e_f = jnp.exp(-jnp.abs(h_f))
    r_f = 1.0 / (1.0 + e_f)
    h_k = jnp.where(h_f >= 0, e_f * r_f, r_f)
    log_fh = jnp.minimum(h_f, 0.0) - jnp.log(1.0 + e_f)
    if layer > 0:
        logits = lbl_ref[...]
        ex = jnp.exp(logits - jnp.max(logits, axis=0, keepdims=True))
        sm = ex / jnp.sum(ex, axis=0, keepdims=True)
        lb = sm[1:2, :]
        for j in range(2, layer + 1):
            lb = lb + sm[j:j + 1, :]
        h_k = (1.0 - lb) * h_k
        lo = jnp.log(lb)
        hi = jnp.log1p(-lb) + log_fh
        log_fh = jnp.maximum(lo, hi) + jnp.log(1.0 + jnp.exp(-jnp.abs(lo - hi)))
    side.step()
    oh = _gla_chunk(_silu(h_q), h_k, h_i, log_fh, h_s, lvl, HEAD_DIM, side)
    y_c = _per_head(oh, _rms_normalize) * vec(ROW_H_NORM_W) * _silu(h_z)

    u = take_in_proj(u_d, COL_D, COL_M)
    proj_2 = queue_branch_proj(2, y_c)
    g_q, g_k, g_v = u[:, :GLA_DK], u[:, GLA_DK:2 * GLA_DK], u[:, 2 * GLA_DK:2 * GLA_DK + W_BRANCH]
    g_lr = u[:, 2 * GLA_DK + W_BRANCH:2 * GLA_DK + W_BRANCH + LANES]
    g_z = u[:, 2 * GLA_DK + W_BRANCH + LANES:]
    log_gk = (_log_sigmoid(_dot(g_lr.astype(BF16), w2_ref[...]) + vec(ROW_G_B_LR2)[:, :GLA_DK])
              * (1.0 / GLA_TAU))
    og = _gla_chunk(g_q, g_k, g_v, log_gk, g_s, lvl, GLA_HEAD_DK, side)
    y_d = _per_head(og, _rms_normalize) * vec(ROW_G_NORM_W) * _silu(g_z)

    side.drain()
    merged = cat(gate[0]) * proj_0[0] + cat(gate[1]) * proj_1[0] + cat(gate[2]) * proj_2[0]
    merged = merged + cat(gate[3]) * _dot(y_d.astype(BF16), wb_ref[3])
    y = x + _dot(merged.astype(BF16), wo_ref[...])
    if final_norm:
        y = _rms_normalize(y) * fg_ref[...]
    o_ref[...] = y


def _block_diag_tiles(w):
    n, b, _ = w.shape
    rows = w.astype(BF16).reshape(n * b // MXU_DIM, MXU_DIM, b)
    col = np.arange(MXU_DIM)
    repeat = (col[None, :] % b == np.arange(b)[:, None]).astype(np.float32)
    repeated = jnp.einsum('trj,jc->trc', rows, jnp.asarray(repeat, BF16), preferred_element_type=F32)
    own_block = col[:, None] // b == col[None, :] // b
    return jnp.where(own_block, repeated, 0.0).astype(BF16)


def _level_table(t):
    idx = np.arange(t)
    xor = idx[:, None] ^ idx[None, :]
    lv = np.floor(np.log2(np.maximum(xor, 1))).astype(np.int32) + 1
    lv = np.where(idx[:, None] == idx[None, :], 0, lv)
    return np.where(idx[:, None] >= idx[None, :], lv, -1).astype(np.float32)


def _causal_bias(t):
    idx = np.arange(t)
    return np.where(idx[:, None] >= idx[None, :], 0.0, -np.inf).astype(np.float32)


def _pack_weights_kernel(depth, w_in_ref, w_br_ref, w_o_ref, *out_refs):
    w_all, w_br, w_o = out_refs[:depth], out_refs[depth:2 * depth], out_refs[2 * depth:]
    lane = lax.broadcasted_iota(jnp.int32, (PACK_ROWS, LANES), 1)
    for l in range(depth):
        w_all[l][:, :COL_LR] = w_in_ref[l, :, :COL_LR].astype(BF16)
        lr_tile = w_in_ref[l, :, COL_LR:COL_LR + LANES]
        w_all[l][:, COL_LR:COL_LR + LANES] = jnp.where(lane < GLA_RANK, lr_tile, 0.0).astype(BF16)
        w_all[l][:, COL_LR + LANES:] = w_in_ref[l, :, COL_LR + GLA_RANK:].astype(BF16)
        w_br[l][...] = w_br_ref[l].astype(BF16)
        w_o[l][...] = w_o_ref[l].astype(BF16)


def _pack_weights(w_in, w_branch, w_out):
    depth, _, d_in = w_in.shape
    assert d_in == sum(SPLIT_SIZES) and COL_END == d_in - GLA_RANK + LANES
    n_steps = D_MODEL // PACK_ROWS
    br_rows = N_BRANCH * W_BRANCH // n_steps

    def row_block(rows, width, layers=None):
        if layers is None:
            return pl.BlockSpec((rows, width), lambda r: (r, 0))
        return pl.BlockSpec((layers, rows, width), lambda r: (0, r, 0))

    outs = pl.pallas_call(
        functools.partial(_pack_weights_kernel, depth),
        grid=(n_steps,),
        in_specs=[row_block(PACK_ROWS, d_in, depth), row_block(br_rows, D_MODEL, depth),
                  row_block(PACK_ROWS, D_MODEL, depth)],
        out_specs=([row_block(PACK_ROWS, COL_END)] * depth + [row_block(br_rows, D_MODEL)] * depth
                   + [row_block(PACK_ROWS, D_MODEL)] * depth),
        out_shape=([jax.ShapeDtypeStruct((D_MODEL, COL_END), BF16)] * depth
                   + [jax.ShapeDtypeStruct((N_BRANCH * W_BRANCH, D_MODEL), BF16)] * depth
                   + [jax.ShapeDtypeStruct((D_MODEL, D_MODEL), BF16)] * depth),
        compiler_params=pltpu.CompilerParams(dimension_semantics=("arbitrary",),
                                             vmem_limit_bytes=VMEM_LIMIT_BYTES),
        name="pack_weights",
    )(w_in, w_branch.reshape(depth, N_BRANCH * W_BRANCH, D_MODEL), w_out)
    w_all, w_br, w_o = outs[:depth], outs[depth:2 * depth], outs[2 * depth:]
    return w_all, [w.reshape(N_BRANCH, W_BRANCH, D_MODEL) for w in w_br], w_o


def _pack_layer(l, lru_conv_w, lru_conv_b, lru_wa, lru_ba, lru_wx, lru_bx, lru_lambda,
                m_conv_w, m_conv_b, m_wq, m_wk, m_wv, m_wi, m_bi, m_wf, m_bf, m_norm_w, m_skip,
                h_norm_w, g_w_lr2, g_b_lr2, g_norm_w):
    lru_w = jnp.concatenate([_block_diag_tiles(lru_wa[l]), _block_diag_tiles(lru_wx[l])], axis=2)
    m_qk = jnp.concatenate([_block_diag_tiles(m_wq[l]), _block_diag_tiles(m_wk[l])], axis=2)
    m_v = _block_diag_tiles(m_wv[l])
    w_gate = jnp.zeros((3 * W_BRANCH, 2 * LANES), F32)
    w_gate = w_gate.at[:, :N_HEADS].set(m_wi[l]).at[:, LANES:LANES + N_HEADS].set(m_wf[l])
    b_gate = jnp.zeros((1, 2 * LANES), F32)
    b_gate = b_gate.at[0, :N_HEADS].set(m_bi[l]).at[0, LANES:LANES + N_HEADS].set(m_bf[l])
    w_lr2 = jnp.pad(g_w_lr2[l], ((0, LANES - GLA_RANK), (0, 0)))
    rows = [lru_conv_w[l][j] for j in range(CONV_WIDTH)]
    rows += [lru_conv_b[l], lru_ba[l], lru_bx[l], lru_lambda[l]]
    rows += [m_conv_w[l][j] for j in range(CONV_WIDTH)]
    rows += [m_conv_b[l], m_norm_w[l], m_skip[l], jnp.tile(h_norm_w[l], N_HEADS), jnp.tile(g_norm_w[l], N_HEADS),
             jnp.pad(g_b_lr2[l], (0, W_BRANCH - GLA_DK))]
    vecs = jnp.stack(rows).astype(F32)
    vecs = jnp.pad(vecs, ((0, N_VEC_ROWS - vecs.shape[0]), (0, 0)))
    return lru_w, m_qk, m_v, w_gate.astype(BF16), b_gate, w_lr2.astype(BF16), vecs


def _layer_call(layer, final_norm, batch, seq):
    n_tiles = seq // TILE
    tile_spec = pl.BlockSpec((TILE, D_MODEL), lambda b, j: (b * n_tiles + j, 0))
    resident = pl.BlockSpec(memory_space=pltpu.VMEM)
    state = functools.partial(pltpu.VMEM, dtype=F32)
    return pl.pallas_call(
        functools.partial(_layer_kernel, layer, final_norm),
        grid=(batch, n_tiles),
        in_specs=[tile_spec] + [resident] * 15,
        out_specs=tile_spec,
        out_shape=jax.ShapeDtypeStruct((batch * seq, D_MODEL), F32),
        scratch_shapes=[
            state((SUBLANES, W_BRANCH)),
            state((SUBLANES, W_BRANCH)),
            state((SUBLANES, W_BRANCH)),
            state((N_HEADS, HEAD_DIM, HEAD_DIM)),
            state((SUBLANES, W_BRANCH)),
            state((SUBLANES, LANES)),
            state((N_HEADS, HEAD_DIM, HEAD_DIM)),
            state((N_HEADS, HEAD_DIM, LANES)),
        ],
        compiler_params=pltpu.CompilerParams(
            dimension_semantics=("arbitrary", "arbitrary"),
            vmem_limit_bytes=VMEM_LIMIT_BYTES,
        ),
        name=f"hybrid_layer_{layer}",
    )


def kernel(x, norm_g, w_in, lru_conv_w, lru_conv_b, lru_wa, lru_ba, lru_wx, lru_bx, lru_lambda, m_conv_w, m_conv_b, m_wq, m_wk, m_wv, m_wi, m_bi, m_wf, m_bf, m_norm_w, m_skip, h_lb_logits, h_norm_w, g_w_lr2, g_b_lr2, g_norm_w, w_branch, w_out, final_g):
    batch, seq, d = x.shape
    depth = w_in.shape[0]
    assert d == D_MODEL and seq % TILE == 0
    lvl = jnp.asarray(_level_table(TILE), BF16)
    causal_bias = jnp.asarray(_causal_bias(TILE))
    lb_logits = h_lb_logits.astype(F32)
    w_all, w_br, w_o = _pack_weights(w_in, w_branch, w_out)
    y = x.reshape(batch * seq, d).astype(F32)
    for l in range(depth):
        packed = _pack_layer(l, lru_conv_w, lru_conv_b, lru_wa, lru_ba, lru_wx, lru_bx, lru_lambda,
                             m_conv_w, m_conv_b, m_wq, m_wk, m_wv, m_wi, m_bi, m_wf, m_bf, m_norm_w, m_skip,
                             h_norm_w, g_w_lr2, g_b_lr2, g_norm_w)
        (lru_w, m_qk, m_v, w_gate, b_gate, w_lr2, vecs) = packed
        y = _layer_call(l, l == depth - 1, batch, seq)(
            y, norm_g[l].reshape(1, d), final_g.reshape(1, d), w_all[l], w_br[l], w_o[l], lru_w, m_qk, m_v, w_gate,
            b_gate, w_lr2, vecs, lb_logits, lvl, causal_bias)
    return y.reshape(batch, seq, d)
```

```python
import collections
import functools

import numpy as np
import jax
import jax.numpy as jnp
from jax import lax
from jax.experimental import pallas as pl
from jax.experimental.pallas import tpu as pltpu

D_MODEL = 1024
W_BRANCH = 512
N_BRANCH = 4
CONV_WIDTH = 4
LRU_BLOCKS = 8
LRU_C = 8.0
N_HEADS = 4
HEAD_DIM = W_BRANCH // N_HEADS
QKV_BLOCK = 4
GLA_DK = 256
GLA_HEAD_DK = GLA_DK // N_HEADS
GLA_RANK = 16
GLA_TAU = 16.0
NORM_EPS = 1e-6
SPLIT_SIZES = (512, 512, 512, 512, 512, 512, 512, 512, 512, 256, 256, 512, 16, 512, 4096)

LANES = 128
SUBLANES = 8
MXU_DIM = 256
VMEM_LIMIT_BYTES = 56 * 1024 * 1024

TILE = 256
PROJ_CHUNK = MXU_DIM

COL_A = 0
COL_B = COL_A + 2 * W_BRANCH
COL_C = COL_B + 3 * W_BRANCH
COL_D = COL_C + 4 * W_BRANCH
COL_LR = COL_D + 2 * GLA_DK + W_BRANCH
COL_M = COL_LR + LANES + W_BRANCH
COL_END = COL_M + N_BRANCH * D_MODEL
PACK_PAD = LANES - GLA_RANK
PACK_COLS = 3 * LANES
PACK_ROWS = 128

ROW_LRU_CONV_W = 0
ROW_LRU_CONV_B = 4
ROW_LRU_BA = 5
ROW_LRU_BX = 6
ROW_LRU_LAMBDA = 7
ROW_M_CONV_W = 8
ROW_M_CONV_B = 12
ROW_M_NORM_W = 13
ROW_M_SKIP = 14
ROW_H_NORM_W = 15
ROW_G_NORM_W = 16
ROW_G_B_LR2 = 17
N_VEC_ROWS = 24

BF16 = jnp.bfloat16
F32 = jnp.float32


class _SideWork:
    def __init__(self):
        self._items = collections.deque()

    def add(self, thunk):
        self._items.append(thunk)

    def step(self, n=1):
        for _ in range(n):
            if self._items:
                self._items.popleft()()

    def run_until(self, pieces, count):
        while len(pieces) < count:
            self._items.popleft()()

    def drain(self):
        while self._items:
            self._items.popleft()()


def _dot(a, b):
    return jnp.dot(a, b, preferred_element_type=F32)


def _dot_nt(a, b):
    return lax.dot_general(a, b, (((1,), (1,)), ((), ())), preferred_element_type=F32)


def _dot_tn(a, b):
    return lax.dot_general(a, b, (((0,), (0,)), ((), ())), preferred_element_type=F32)


def _sigmoid(x):
    return jax.nn.sigmoid(x)


def _silu(x):
    return x * _sigmoid(x)


def _log_sigmoid(x):
    return jnp.minimum(x, 0.0) - jnp.log(1.0 + jnp.exp(-jnp.abs(x)))


def _softplus(x):
    return jnp.maximum(x, 0.0) + jnp.log(1.0 + jnp.exp(-jnp.abs(x)))


def _to_tiles(x):
    t, c = x.shape
    return x.reshape(t // SUBLANES, SUBLANES, c)


def _from_tiles(x3):
    nb, s, c = x3.shape
    return x3.reshape(nb * s, c)


def _sub_iota(shape):
    return lax.broadcasted_iota(jnp.int32, shape, 1)


def _block_ref(x, b):
    t, c = x.shape
    if b == 1:
        x3 = _to_tiles(x)
        return _from_tiles(jnp.where((_sub_iota(x3.shape) & 1) == 1, pltpu.roll(x3, 1, 1), x3))
    if b == 2:
        x3 = _to_tiles(x)
        m = _sub_iota(x3.shape) & 3
        return _from_tiles(jnp.where(m == 0, pltpu.roll(x3, SUBLANES - 1, 1),
                                     jnp.where(m == 1, x3,
                                               jnp.where(m == 2, pltpu.roll(x3, 1, 1), pltpu.roll(x3, 2, 1)))))
    nb = t // (2 * b)
    x3 = x.reshape(nb, 2 * b, c)
    return jnp.broadcast_to(x3[:, b - 1:b, :], (nb, 2 * b, c)).reshape(t, c)


def _prefix_scan_rows(x, combine, identity):
    x3 = _to_tiles(x)
    nb, _, c = x3.shape
    j = _sub_iota(x3.shape)
    for k in (1, 2, 4):
        x3 = combine(jnp.where(j >= k, pltpu.roll(x3, k, 1), identity), x3)
    totals = x3[:, SUBLANES - 1:, :]
    carry = jnp.full((1, 1, c), identity, x.dtype)
    carries = []
    for i in range(nb):
        carries.append(carry)
        carry = combine(carry, totals[i:i + 1])
    return _from_tiles(combine(jnp.concatenate(carries, axis=0), x3))


def _linear_recurrence_rows(a, u, h_prev, side):
    a3, u3 = _to_tiles(a), _to_tiles(u)
    nb, _, c = a3.shape
    j = _sub_iota(a3.shape)
    for k in (1, 2, 4):
        a_back = jnp.where(j >= k, pltpu.roll(a3, k, 1), 1.0)
        u_back = jnp.where(j >= k, pltpu.roll(u3, k, 1), 0.0)
        u3 = a3 * u_back + u3
        a3 = a3 * a_back
        side.step(2)
    a_tot, u_tot = a3[:, SUBLANES - 1:, :], u3[:, SUBLANES - 1:, :]
    h_in = h_prev.reshape(1, 1, c)
    carries = []
    for i in range(nb):
        carries.append(h_in)
        h_in = u_tot[i:i + 1] + a_tot[i:i + 1] * h_in
    return _from_tiles(u3 + a3 * jnp.concatenate(carries, axis=0))


def _causal_conv(x, tail_ref, w, bias):
    t, c = x.shape
    x3 = _to_tiles(x)
    prev3 = jnp.concatenate([tail_ref[...].reshape(1, SUBLANES, c), x3[:-1]], axis=0)
    j = _sub_iota(x3.shape)
    y = bias + w[CONV_WIDTH - 1] * x
    for back in range(1, CONV_WIDTH):
        shifted = pltpu.roll(jnp.where(j >= SUBLANES - back, prev3, x3), back, 1)
        y = y + w[CONV_WIDTH - 1 - back] * _from_tiles(shifted)
    tail_ref[...] = x[t - SUBLANES:, :]
    return y


def _head_slices():
    return [slice(h * HEAD_DIM, (h + 1) * HEAD_DIM) for h in range(N_HEADS)]


def _gla_chunk(q, k, v, g, st_ref, lvl, head_dk, side):
    t = q.shape[0]
    per_tile = LANES // head_dk
    lane = lax.broadcasted_iota(jnp.int32, (1, LANES), 1)

    def key_tile(x, h):
        tile = h // per_tile
        return x[:, tile * LANES:(tile + 1) * LANES]

    def query_tile(x, h):
        if per_tile == 1:
            return key_tile(x, h)
        own = ((lane // head_dk) == (h % per_tile)).astype(BF16)
        return key_tile(x, h) * own

    q = q * (head_dk ** -0.5)
    vb = v.astype(BF16)
    qb = q.astype(BF16)
    kb = k.astype(BF16)
    heads = range(N_HEADS)
    zero = jnp.zeros((), BF16)

    def diagonal(h):
        prod = key_tile(q, h) * key_tile(k, h)
        if per_tile > 1:
            prod = jnp.where((lane // head_dk) == (h % per_tile), prod, 0.0)
        return jnp.broadcast_to(jnp.sum(prod, axis=-1, keepdims=True), (t, t)).astype(BF16)

    scores = [jnp.where(lvl == 0, diagonal(h), zero) for h in heads]
    g_cum = _prefix_scan_rows(g, jnp.add, 0.0)
    b, level = 1, 1
    while b < t:
        eb = jnp.exp(-jnp.abs(g_cum - _block_ref(g_cum, b))).astype(BF16)
        qe = qb * eb
        ke = kb * eb
        scores = [jnp.where(lvl == level, _dot_nt(query_tile(qe, h), key_tile(ke, h)).astype(BF16), s)
                  for h, s in zip(heads, scores)]
        side.step()
        b *= 2
        level += 1
    g_last = g_cum[t - 1:t, :]
    q_in = (q * jnp.exp(g_cum)).astype(BF16)
    k_out = (k * jnp.exp(g_last - g_cum)).astype(BF16)
    state_decay = jnp.exp(g_last)
    outs = []
    for h in heads:
        sl = slice(h * HEAD_DIM, (h + 1) * HEAD_DIM)
        st = st_ref[h]
        outs.append(_dot(scores[h], vb[:, sl]) + _dot_nt(query_tile(q_in, h), st.astype(BF16)))
        st_ref[h] = st * key_tile(state_decay, h) + _dot_tn(vb[:, sl], key_tile(k_out, h))
        side.step()
    return jnp.concatenate(outs, axis=1)


def _mlstm_chunk(q, k, v, log_i, log_f, c_ref, n_ref, m_ref, causal_bias, side):
    t = q.shape[0]
    heads = _head_slices()
    b = _prefix_scan_rows(log_f, jnp.add, 0.0)
    z = log_i - b
    zmax = _prefix_scan_rows(z, jnp.maximum, -jnp.inf)
    m_prev = m_ref[0:1, :]
    m_t = b + jnp.maximum(m_prev, zmax)
    col = b - m_t
    w_inter = jnp.exp(b + m_prev - m_t)
    inv_floor = jnp.exp(-m_t)
    z_rows = z.T
    b_last = b[t - 1:t, :]
    m_new = m_t[t - 1:t, :]
    w_state = jnp.exp(z + b_last - m_new)
    decay = jnp.exp(b_last + m_prev - m_new)
    qs = q * (HEAD_DIM ** -0.5)
    qb = qs.astype(BF16)
    kb = k.astype(BF16)
    vb = v.astype(BF16)
    n_prev = n_ref[0:1, :]
    outs, n_new = [], []
    for h, sl in enumerate(heads):
        hl = slice(h, h + 1)
        dm = col[:, hl] + z_rows[hl, :] + causal_bias
        s = _dot_nt(qb[:, sl], kb[:, sl]) * jnp.exp(dm)
        c_h = c_ref[h]
        wi = w_inter[:, hl]
        num = _dot(s.astype(BF16), vb[:, sl]) + wi * _dot(qb[:, sl], c_h.astype(BF16))
        den = (jnp.sum(s, axis=-1, keepdims=True)
               + wi * jnp.sum(qs[:, sl] * n_prev[:, sl], axis=-1, keepdims=True))
        outs.append(num / jnp.maximum(jnp.abs(den), inv_floor[:, hl]))
        kw = k[:, sl] * w_state[:, hl]
        d_h = decay[:, hl]
        c_ref[h] = d_h * c_h + _dot_tn(kw.astype(BF16), vb[:, sl])
        n_new.append(d_h * n_prev[:, sl] + jnp.sum(kw, axis=0, keepdims=True))
        side.step(2)
    n_ref[0:1, :] = jnp.concatenate(n_new, axis=1)
    m_ref[0:1, :] = m_new
    return jnp.concatenate(outs, axis=1)


def _per_head(x, fn):
    return jnp.concatenate([fn(x[:, sl]) for sl in _head_slices()], axis=1)


def _rms_normalize(x):
    return x * lax.rsqrt(jnp.mean(x * x, axis=-1, keepdims=True) + NORM_EPS)


def _layer_normalize(x):
    xc = x - jnp.mean(x, axis=-1, keepdims=True)
    return xc * lax.rsqrt(jnp.mean(xc * xc, axis=-1, keepdims=True) + NORM_EPS)


def _layer_kernel(layer, final_norm,
                  x_ref, ng_ref, fg_ref, w_ref, wb_ref, wo_ref, lruw_ref, mqk_ref, mv_ref, wg_ref, bg_ref,
                  w2_ref, vec_ref, lbl_ref, lvl_ref, cb_ref,
                  o_ref,
                  conv_a, conv_b, lru_h, m_c, m_n, m_m, h_s, g_s):
    half = MXU_DIM

    @pl.when(pl.program_id(1) == 0)
    def _reset_state():
        conv_a[...] = jnp.zeros(conv_a.shape, F32)
        conv_b[...] = jnp.zeros(conv_b.shape, F32)
        lru_h[...] = jnp.zeros(lru_h.shape, F32)
        m_c[...] = jnp.zeros(m_c.shape, F32)
        m_n[...] = jnp.zeros(m_n.shape, F32)
        m_m[...] = jnp.zeros(m_m.shape, F32)
        h_s[...] = jnp.zeros(h_s.shape, F32)
        g_s[...] = jnp.zeros(g_s.shape, F32)

    def vec(row):
        return vec_ref[row:row + 1, :]

    lvl = lvl_ref[...]
    x = x_ref[...]
    hb = (_rms_normalize(x) * ng_ref[...]).astype(BF16)
    side = _SideWork()

    def cat(pieces):
        return jnp.concatenate(pieces, axis=1)

    def queue_in_proj(lo, hi):
        pieces = []
        for c in range(lo, hi, PROJ_CHUNK):
            side.add(lambda c=c: pieces.append(_dot(hb, w_ref[:, c:min(c + PROJ_CHUNK, hi)])))
        return pieces

    def take_in_proj(pieces, lo, hi):
        side.run_until(pieces, -(-(hi - lo) // PROJ_CHUNK))
        return cat(pieces)

    def queue_merge_gate(branch):
        pieces = []
        base = COL_M + branch * D_MODEL
        for c in range(base, base + D_MODEL, PROJ_CHUNK):
            side.add(lambda c=c: pieces.append(_sigmoid(_dot(hb, w_ref[:, c:c + PROJ_CHUNK]))))
        return pieces

    def queue_branch_proj(branch, y):
        out = []
        yb = y.astype(BF16)
        side.add(lambda: out.append(_dot(yb, wb_ref[branch])))
        return out

    u = _dot(hb, w_ref[:, COL_A:COL_B])
    u_b, u_c, u_d = queue_in_proj(COL_B, COL_C), queue_in_proj(COL_C, COL_D), queue_in_proj(COL_D, COL_M)
    gate = [queue_merge_gate(n) for n in range(N_BRANCH)]

    lru_x, lru_z = u[:, :W_BRANCH], u[:, W_BRANCH:]
    xa = _causal_conv(lru_x, conv_a, [vec(ROW_LRU_CONV_W + j) for j in range(CONV_WIDTH)], vec(ROW_LRU_CONV_B))
    xab = xa.astype(BF16)
    g0 = _dot(xab[:, :half], lruw_ref[0])
    g1 = _dot(xab[:, half:], lruw_ref[1])
    side.step(2)
    r = _sigmoid(jnp.concatenate([g0[:, :half], g1[:, :half]], axis=1) + vec(ROW_LRU_BA))
    i = _sigmoid(jnp.concatenate([g0[:, half:], g1[:, half:]], axis=1) + vec(ROW_LRU_BX))
    log_a = (-LRU_C) * r * _softplus(-vec(ROW_LRU_LAMBDA))
    a = jnp.exp(log_a)
    side.step(2)
    h_lru = _linear_recurrence_rows(a, jnp.sqrt(1.0 - a * a) * (i * xa),
                                    lru_h[SUBLANES - 1:SUBLANES, :], side)
    lru_h[...] = h_lru[TILE - SUBLANES:, :]
    y_a = h_lru * _silu(lru_z)

    u = take_in_proj(u_b, COL_B, COL_C)
    proj_0 = queue_branch_proj(0, y_a)
    m_x, m_o, m_z = u[:, :W_BRANCH], u[:, W_BRANCH:2 * W_BRANCH], u[:, 2 * W_BRANCH:]
    xm = _silu(_causal_conv(m_x, conv_b, [vec(ROW_M_CONV_W + j) for j in range(CONV_WIDTH)], vec(ROW_M_CONV_B)))
    xmb = xm.astype(BF16)
    mxb = m_x.astype(BF16)
    qk0 = _dot(xmb[:, :half], mqk_ref[0])
    qk1 = _dot(xmb[:, half:], mqk_ref[1])
    q = jnp.concatenate([qk0[:, :half], qk1[:, :half]], axis=1)
    k = jnp.concatenate([qk0[:, half:], qk1[:, half:]], axis=1)
    v = jnp.concatenate([_dot(mxb[:, :half], mv_ref[0]), _dot(mxb[:, half:], mv_ref[1])], axis=1)
    side.step()
    gates = _dot(jnp.concatenate([q, k, v], axis=1).astype(BF16), wg_ref[...]) + bg_ref[...]
    hm = _mlstm_chunk(q, k, v, gates[:, :LANES], _log_sigmoid(gates[:, LANES:]), m_c, m_n, m_m, cb_ref[...], side)
    hm = _sigmoid(m_o) * hm
    y_b = (_per_head(hm, _layer_normalize) * vec(ROW_M_NORM_W) + vec(ROW_M_SKIP) * xm) * _silu(m_z)

    u = take_in_proj(u_c, COL_C, COL_D)
    proj_1 = queue_branch_proj(1, y_b)
    h_q, h_f = u[:, :W_BRANCH], u[:, W_BRANCH:2 * W_BRANCH]
    h_i, h_z = u[:, 2 * W_BRANCH:3 * W_BRANCH], u[:, 3 * W_BRANCH:]
    e_f = jnp.exp(-jnp.abs(h_f))
    r_f = 1.0 / (1.0 + e_f)
    h_k = jnp.where(h_f >= 0, e_f * r_f, r_f)
    log_fh = jnp.minimum(h_f, 0.0) - jnp.log(1.0 + e_f)
    if layer > 0:
        logits = lbl_ref[...]
        ex = jnp.exp(logits - jnp.max(logits, axis=0, keepdims=True))
        sm = ex / jnp.sum(ex, axis=0, keepdims=True)
        lb = sm[1:2, :]
        for j in range(2, layer + 1):
            lb = lb + sm[j:j + 1, :]
        h_k = (1.0 - lb) * h_k
        lo = jnp.log(lb)
        hi = jnp.log1p(-lb) + log_fh
        log_fh = jnp.maximum(lo, hi) + jnp.log(1.0 + jnp.exp(-jnp.abs(lo - hi)))
    side.step()
    oh = _gla_chunk(_silu(h_q), h_k, h_i, log_fh, h_s, lvl, HEAD_DIM, side)
    y_c = _per_head(oh, _rms_normalize) * vec(ROW_H_NORM_W) * _silu(h_z)

    u = take_in_proj(u_d, COL_D, COL_M)
    proj_2 = queue_branch_proj(2, y_c)
    g_q, g_k, g_v = u[:, :GLA_DK], u[:, GLA_DK:2 * GLA_DK], u[:, 2 * GLA_DK:2 * GLA_DK + W_BRANCH]
    g_lr = u[:, 2 * GLA_DK + W_BRANCH:2 * GLA_DK + W_BRANCH + LANES]
    g_z = u[:, 2 * GLA_DK + W_BRANCH + LANES:]
    log_gk = (_log_sigmoid(_dot(g_lr.astype(BF16), w2_ref[...]) + vec(ROW_G_B_LR2)[:, :GLA_DK])
              * (1.0 / GLA_TAU))
    og = _gla_chunk(g_q, g_k, g_v, log_gk, g_s, lvl, GLA_HEAD_DK, side)
    y_d = _per_head(og, _rms_normalize) * vec(ROW_G_NORM_W) * _silu(g_z)

    side.drain()
    merged = cat(gate[0]) * proj_0[0] + cat(gate[1]) * proj_1[0] + cat(gate[2]) * proj_2[0]
    merged = merged + cat(gate[3]) * _dot(y_d.astype(BF16), wb_ref[3])
    y = x + _dot(merged.astype(BF16), wo_ref[...])
    if final_norm:
        y = _rms_normalize(y) * fg_ref[...]
    o_ref[...] = y


def _block_diag_tiles(w):
    n, b, _ = w.shape
    rows = w.astype(BF16).reshape(n * b // MXU_DIM, MXU_DIM, b)
    col = np.arange(MXU_DIM)
    repeat = (col[None, :] % b == np.arange(b)[:, None]).astype(np.float32)
    repeated = jnp.einsum('trj,jc->trc', rows, jnp.asarray(repeat, BF16), preferred_element_type=F32)
    own_block = col[:, None] // b == col[None, :] // b
    return jnp.where(own_block, repeated, 0.0).astype(BF16)


def _level_table(t):
    idx = np.arange(t)
    xor = idx[:, None] ^ idx[None, :]
    lv = np.floor(np.log2(np.maximum(xor, 1))).astype(np.int32) + 1
    lv = np.where(idx[:, None] == idx[None, :], 0, lv)
    return np.where(idx[:, None] >= idx[None, :], lv, -1).astype(np.float32)


def _causal_bias(t):
    idx = np.arange(t)
    return np.where(idx[:, None] >= idx[None, :], 0.0, -np.inf).astype(np.float32)


def _pack_in_proj_kernel(depth, wt_ref, *refs):
    outs, carry = refs[:depth], refs[depth]
    j = pl.program_id(0)
    lr_block = (COL_LR + LANES) // PACK_COLS - 1
    real_rows = COL_LR + GLA_RANK - lr_block * PACK_COLS
    row = lax.broadcasted_iota(jnp.int32, (PACK_COLS, D_MODEL), 0)
    for l in range(depth):
        x = wt_ref[l]
        shifted = jnp.concatenate([carry[l], x[:PACK_COLS - PACK_PAD]], axis=0)
        padded = jnp.where(row < real_rows, x, 0.0)
        y = jnp.where(j > lr_block, shifted, jnp.where(j == lr_block, padded, x))
        outs[l][...] = y.T.astype(BF16)
        carry[l] = x[PACK_COLS - PACK_PAD:]


def _pack_in_proj(w_in):
    depth, _, d_in = w_in.shape
    assert d_in == sum(SPLIT_SIZES) and COL_END == d_in + PACK_PAD
    assert COL_END % PACK_COLS == 0 and (COL_LR + LANES) % PACK_COLS == 0
    last_block = (d_in - 1) // PACK_COLS
    return pl.pallas_call(
        functools.partial(_pack_in_proj_kernel, depth),
        grid=(COL_END // PACK_COLS,),
        in_specs=[pl.BlockSpec((depth, PACK_COLS, D_MODEL), lambda j: (0, jnp.minimum(j, last_block), 0))],
        out_specs=[pl.BlockSpec((D_MODEL, PACK_COLS), lambda j: (0, j))] * depth,
        out_shape=[jax.ShapeDtypeStruct((D_MODEL, COL_END), BF16)] * depth,
        scratch_shapes=[pltpu.VMEM((depth, PACK_PAD, D_MODEL), F32)],
        compiler_params=pltpu.CompilerParams(dimension_semantics=("arbitrary",),
                                             vmem_limit_bytes=VMEM_LIMIT_BYTES),
        name="pack_in_proj",
    )(jnp.swapaxes(w_in, 1, 2))


def _pack_out_proj_kernel(depth, w_br_ref, w_o_ref, *out_refs):
    for l in range(depth):
        out_refs[l][...] = w_br_ref[l].astype(BF16)
        out_refs[depth + l][...] = w_o_ref[l].astype(BF16)


def _pack_out_proj(w_branch, w_out):
    depth = w_branch.shape[0]
    n_steps = D_MODEL // PACK_ROWS
    br_rows = N_BRANCH * W_BRANCH // n_steps
    outs = pl.pallas_call(
        functools.partial(_pack_out_proj_kernel, depth),
        grid=(n_steps,),
        in_specs=[pl.BlockSpec((depth, br_rows, D_MODEL), lambda r: (0, r, 0)),
                  pl.BlockSpec((depth, PACK_ROWS, D_MODEL), lambda r: (0, r, 0))],
        out_specs=([pl.BlockSpec((br_rows, D_MODEL), lambda r: (r, 0))] * depth
                   + [pl.BlockSpec((PACK_ROWS, D_MODEL), lambda r: (r, 0))] * depth),
        out_shape=([jax.ShapeDtypeStruct((N_BRANCH * W_BRANCH, D_MODEL), BF16)] * depth
                   + [jax.ShapeDtypeStruct((D_MODEL, D_MODEL), BF16)] * depth),
        compiler_params=pltpu.CompilerParams(dimension_semantics=("arbitrary",)),
        name="pack_out_proj",
    )(w_branch.reshape(depth, N_BRANCH * W_BRANCH, D_MODEL), w_out)
    return [w.reshape(N_BRANCH, W_BRANCH, D_MODEL) for w in outs[:depth]], outs[depth:]


def _pack_layer(l, lru_conv_w, lru_conv_b, lru_wa, lru_ba, lru_wx, lru_bx, lru_lambda,
                m_conv_w, m_conv_b, m_wq, m_wk, m_wv, m_wi, m_bi, m_wf, m_bf, m_norm_w, m_skip,
                h_norm_w, g_w_lr2, g_b_lr2, g_norm_w):
    lru_w = jnp.concatenate([_block_diag_tiles(lru_wa[l]), _block_diag_tiles(lru_wx[l])], axis=2)
    m_qk = jnp.concatenate([_block_diag_tiles(m_wq[l]), _block_diag_tiles(m_wk[l])], axis=2)
    m_v = _block_diag_tiles(m_wv[l])
    w_gate = jnp.zeros((3 * W_BRANCH, 2 * LANES), F32)
    w_gate = w_gate.at[:, :N_HEADS].set(m_wi[l]).at[:, LANES:LANES + N_HEADS].set(m_wf[l])
    b_gate = jnp.zeros((1, 2 * LANES), F32)
    b_gate = b_gate.at[0, :N_HEADS].set(m_bi[l]).at[0, LANES:LANES + N_HEADS].set(m_bf[l])
    w_lr2 = jnp.pad(g_w_lr2[l], ((0, LANES - GLA_RANK), (0, 0)))
    rows = [lru_conv_w[l][j] for j in range(CONV_WIDTH)]
    rows += [lru_conv_b[l], lru_ba[l], lru_bx[l], lru_lambda[l]]
    rows += [m_conv_w[l][j] for j in range(CONV_WIDTH)]
    rows += [m_conv_b[l], m_norm_w[l], m_skip[l], jnp.tile(h_norm_w[l], N_HEADS), jnp.tile(g_norm_w[l], N_HEADS),
             jnp.pad(g_b_lr2[l], (0, W_BRANCH - GLA_DK))]
    vecs = jnp.stack(rows).astype(F32)
    vecs = jnp.pad(vecs, ((0, N_VEC_ROWS - vecs.shape[0]), (0, 0)))
    return lru_w, m_qk, m_v, w_gate.astype(BF16), b_gate, w_lr2.astype(BF16), vecs


def _layer_call(layer, final_norm, batch, seq):
    n_tiles = seq // TILE
    tile_spec = pl.BlockSpec((TILE, D_MODEL), lambda b, j: (b * n_tiles + j, 0))
    resident = pl.BlockSpec(memory_space=pltpu.VMEM)
    state = functools.partial(pltpu.VMEM, dtype=F32)
    return pl.pallas_call(
        functools.partial(_layer_kernel, layer, final_norm),
        grid=(batch, n_tiles),
        in_specs=[tile_spec] + [resident] * 15,
        out_specs=tile_spec,
        out_shape=jax.ShapeDtypeStruct((batch * seq, D_MODEL), F32),
        scratch_shapes=[
            state((SUBLANES, W_BRANCH)),
            state((SUBLANES, W_BRANCH)),
            state((SUBLANES, W_BRANCH)),
            state((N_HEADS, HEAD_DIM, HEAD_DIM)),
            state((SUBLANES, W_BRANCH)),
            state((SUBLANES, LANES)),
            state((N_HEADS, HEAD_DIM, HEAD_DIM)),
            state((N_HEADS, HEAD_DIM, LANES)),
        ],
        compiler_params=pltpu.CompilerParams(
            dimension_semantics=("arbitrary", "arbitrary"),
            vmem_limit_bytes=VMEM_LIMIT_BYTES,
        ),
        name=f"hybrid_layer_{layer}",
    )


def kernel(x, norm_g, w_in, lru_conv_w, lru_conv_b, lru_wa, lru_ba, lru_wx, lru_bx, lru_lambda, m_conv_w, m_conv_b, m_wq, m_wk, m_wv, m_wi, m_bi, m_wf, m_bf, m_norm_w, m_skip, h_lb_logits, h_norm_w, g_w_lr2, g_b_lr2, g_norm_w, w_branch, w_out, final_g):
    batch, seq, d = x.shape
    depth = w_in.shape[0]
    assert d == D_MODEL and seq % TILE == 0
    lvl = jnp.asarray(_level_table(TILE), BF16)
    causal_bias = jnp.asarray(_causal_bias(TILE))
    lb_logits = h_lb_logits.astype(F32)
    w_all = _pack_in_proj(w_in)
    w_br, w_o = _pack_out_proj(w_branch, w_out)
    y = x.reshape(batch * seq, d).astype(F32)
    for l in range(depth):
        packed = _pack_layer(l, lru_conv_w, lru_conv_b, lru_wa, lru_ba, lru_wx, lru_bx, lru_lambda,
                             m_conv_w, m_conv_b, m_wq, m_wk, m_wv, m_wi, m_bi, m_wf, m_bf, m_norm_w, m_skip,
                             h_norm_w, g_w_lr2, g_b_lr2, g_norm_w)
        (lru_w, m_qk, m_v, w_gate, b_gate, w_lr2, vecs) = packed
        y = _layer_call(l, l == depth - 1, batch, seq)(
            y, norm_g[l].reshape(1, d), final_g.reshape(1, d), w_all[l], w_br[l], w_o[l], lru_w, m_qk, m_v, w_gate,
            b_gate, w_lr2, vecs, lb_logits, lvl, causal_bias)
    return y.reshape(batch, seq, d)
```

```python
import collections
import functools

import numpy as np
import jax
import jax.numpy as jnp
from jax import lax
from jax.experimental import pallas as pl
from jax.experimental.pallas import tpu as pltpu

D_MODEL = 1024
W_BRANCH = 512
N_BRANCH = 4
CONV_WIDTH = 4
LRU_BLOCKS = 8
LRU_C = 8.0
N_HEADS = 4
HEAD_DIM = W_BRANCH // N_HEADS
QKV_BLOCK = 4
GLA_DK = 256
GLA_HEAD_DK = GLA_DK // N_HEADS
GLA_RANK = 16
GLA_TAU = 16.0
NORM_EPS = 1e-6
SPLIT_SIZES = (512, 512, 512, 512, 512, 512, 512, 512, 512, 256, 256, 512, 16, 512, 4096)

LANES = 128
SUBLANES = 8
MXU_DIM = 256
VMEM_LIMIT_BYTES = 56 * 1024 * 1024

TILE = 256
TILES_PER_STEP = 2
PROJ_CHUNK = MXU_DIM

COL_A = 0
COL_B = COL_A + 2 * W_BRANCH
COL_C = COL_B + 3 * W_BRANCH
COL_D = COL_C + 4 * W_BRANCH
COL_LR = COL_D + 2 * GLA_DK + W_BRANCH
COL_M = COL_LR + LANES + W_BRANCH
COL_END = COL_M + N_BRANCH * D_MODEL
PACK_PAD = LANES - GLA_RANK
PACK_COLS = 3 * LANES
PACK_ROWS = 128

ROW_LRU_CONV_W = 0
ROW_LRU_CONV_B = 4
ROW_LRU_BA = 5
ROW_LRU_BX = 6
ROW_LRU_LAMBDA = 7
ROW_M_CONV_W = 8
ROW_M_CONV_B = 12
ROW_M_NORM_W = 13
ROW_M_SKIP = 14
ROW_H_NORM_W = 15
ROW_G_NORM_W = 16
ROW_G_B_LR2 = 17
N_VEC_ROWS = 24

BF16 = jnp.bfloat16
F32 = jnp.float32


class _SideWork:
    def __init__(self):
        self._items = collections.deque()

    def add(self, thunk):
        self._items.append(thunk)

    def step(self, n=1):
        for _ in range(n):
            if self._items:
                self._items.popleft()()

    def run_until(self, pieces, count):
        while len(pieces) < count:
            self._items.popleft()()

    def drain(self):
        while self._items:
            self._items.popleft()()


def _dot(a, b):
    return jnp.dot(a, b, preferred_element_type=F32)


def _dot_nt(a, b):
    return lax.dot_general(a, b, (((1,), (1,)), ((), ())), preferred_element_type=F32)


def _dot_tn(a, b):
    return lax.dot_general(a, b, (((0,), (0,)), ((), ())), preferred_element_type=F32)


def _sigmoid(x):
    return jax.nn.sigmoid(x)


def _silu(x):
    return x * _sigmoid(x)


def _log_sigmoid(x):
    return jnp.minimum(x, 0.0) - jnp.log(1.0 + jnp.exp(-jnp.abs(x)))


def _softplus(x):
    return jnp.maximum(x, 0.0) + jnp.log(1.0 + jnp.exp(-jnp.abs(x)))


def _to_tiles(x):
    t, c = x.shape
    return x.reshape(t // SUBLANES, SUBLANES, c)


def _from_tiles(x3):
    nb, s, c = x3.shape
    return x3.reshape(nb * s, c)


def _sub_iota(shape):
    return lax.broadcasted_iota(jnp.int32, shape, 1)


def _block_ref(x, b):
    t, c = x.shape
    if b == 1:
        x3 = _to_tiles(x)
        return _from_tiles(jnp.where((_sub_iota(x3.shape) & 1) == 1, pltpu.roll(x3, 1, 1), x3))
    if b == 2:
        x3 = _to_tiles(x)
        m = _sub_iota(x3.shape) & 3
        return _from_tiles(jnp.where(m == 0, pltpu.roll(x3, SUBLANES - 1, 1),
                                     jnp.where(m == 1, x3,
                                               jnp.where(m == 2, pltpu.roll(x3, 1, 1), pltpu.roll(x3, 2, 1)))))
    nb = t // (2 * b)
    x3 = x.reshape(nb, 2 * b, c)
    return jnp.broadcast_to(x3[:, b - 1:b, :], (nb, 2 * b, c)).reshape(t, c)


def _prefix_scan_rows(x, combine, identity):
    x3 = _to_tiles(x)
    nb, _, c = x3.shape
    j = _sub_iota(x3.shape)
    for k in (1, 2, 4):
        x3 = combine(jnp.where(j >= k, pltpu.roll(x3, k, 1), identity), x3)
    totals = x3[:, SUBLANES - 1:, :]
    carry = jnp.full((1, 1, c), identity, x.dtype)
    carries = []
    for i in range(nb):
        carries.append(carry)
        carry = combine(carry, totals[i:i + 1])
    return _from_tiles(combine(jnp.concatenate(carries, axis=0), x3))


def _linear_recurrence_rows(a, u, h_prev, side):
    a3, u3 = _to_tiles(a), _to_tiles(u)
    nb, _, c = a3.shape
    j = _sub_iota(a3.shape)
    for k in (1, 2, 4):
        a_back = jnp.where(j >= k, pltpu.roll(a3, k, 1), 1.0)
        u_back = jnp.where(j >= k, pltpu.roll(u3, k, 1), 0.0)
        u3 = a3 * u_back + u3
        a3 = a3 * a_back
        side.step(2)
    a_tot, u_tot = a3[:, SUBLANES - 1:, :], u3[:, SUBLANES - 1:, :]
    h_in = h_prev.reshape(1, 1, c)
    carries = []
    for i in range(nb):
        carries.append(h_in)
        h_in = u_tot[i:i + 1] + a_tot[i:i + 1] * h_in
    return _from_tiles(u3 + a3 * jnp.concatenate(carries, axis=0))


def _causal_conv(x, tail_ref, w, bias):
    t, c = x.shape
    x3 = _to_tiles(x)
    prev3 = jnp.concatenate([tail_ref[...].reshape(1, SUBLANES, c), x3[:-1]], axis=0)
    j = _sub_iota(x3.shape)
    y = bias + w[CONV_WIDTH - 1] * x
    for back in range(1, CONV_WIDTH):
        shifted = pltpu.roll(jnp.where(j >= SUBLANES - back, prev3, x3), back, 1)
        y = y + w[CONV_WIDTH - 1 - back] * _from_tiles(shifted)
    tail_ref[...] = x[t - SUBLANES:, :]
    return y


def _head_slices():
    return [slice(h * HEAD_DIM, (h + 1) * HEAD_DIM) for h in range(N_HEADS)]


def _gla_chunk(q, k, v, g, st_ref, lvl, head_dk, side):
    t = q.shape[0]
    per_tile = LANES // head_dk
    lane = lax.broadcasted_iota(jnp.int32, (1, LANES), 1)

    def key_tile(x, h):
        tile = h // per_tile
        return x[:, tile * LANES:(tile + 1) * LANES]

    def query_tile(x, h):
        if per_tile == 1:
            return key_tile(x, h)
        own = ((lane // head_dk) == (h % per_tile)).astype(BF16)
        return key_tile(x, h) * own

    q = q * (head_dk ** -0.5)
    vb = v.astype(BF16)
    qb = q.astype(BF16)
    kb = k.astype(BF16)
    heads = range(N_HEADS)
    zero = jnp.zeros((), BF16)

    def diagonal(h):
        prod = key_tile(q, h) * key_tile(k, h)
        if per_tile > 1:
            prod = jnp.where((lane // head_dk) == (h % per_tile), prod, 0.0)
        return jnp.broadcast_to(jnp.sum(prod, axis=-1, keepdims=True), (t, t)).astype(BF16)

    scores = [jnp.where(lvl == 0, diagonal(h), zero) for h in heads]
    g_cum = _prefix_scan_rows(g, jnp.add, 0.0)
    b, level = 1, 1
    while b < t:
        eb = jnp.exp(-jnp.abs(g_cum - _block_ref(g_cum, b))).astype(BF16)
        qe = qb * eb
        ke = kb * eb
        scores = [jnp.where(lvl == level, _dot_nt(query_tile(qe, h), key_tile(ke, h)).astype(BF16), s)
                  for h, s in zip(heads, scores)]
        side.step()
        b *= 2
        level += 1
    g_last = g_cum[t - 1:t, :]
    q_in = (q * jnp.exp(g_cum)).astype(BF16)
    k_out = (k * jnp.exp(g_last - g_cum)).astype(BF16)
    state_decay = jnp.exp(g_last)
    outs = []
    for h in heads:
        sl = slice(h * HEAD_DIM, (h + 1) * HEAD_DIM)
        st = st_ref[h]
        outs.append(_dot(scores[h], vb[:, sl]) + _dot_nt(query_tile(q_in, h), st.astype(BF16)))
        st_ref[h] = st * key_tile(state_decay, h) + _dot_tn(vb[:, sl], key_tile(k_out, h))
        side.step()
    return jnp.concatenate(outs, axis=1)


def _mlstm_chunk(q, k, v, log_i, log_f, c_ref, n_ref, m_ref, causal_bias, side):
    t = q.shape[0]
    heads = _head_slices()
    b = _prefix_scan_rows(log_f, jnp.add, 0.0)
    z = log_i - b
    zmax = _prefix_scan_rows(z, jnp.maximum, -jnp.inf)
    m_prev = m_ref[0:1, :]
    m_t = b + jnp.maximum(m_prev, zmax)
    col = b - m_t
    w_inter = jnp.exp(b + m_prev - m_t)
    inv_floor = jnp.exp(-m_t)
    z_rows = z.T
    b_last = b[t - 1:t, :]
    m_new = m_t[t - 1:t, :]
    w_state = jnp.exp(z + b_last - m_new)
    decay = jnp.exp(b_last + m_prev - m_new)
    qs = q * (HEAD_DIM ** -0.5)
    qb = qs.astype(BF16)
    kb = k.astype(BF16)
    vb = v.astype(BF16)
    n_prev = n_ref[0:1, :]
    outs, n_new = [], []
    for h, sl in enumerate(heads):
        hl = slice(h, h + 1)
        dm = col[:, hl] + z_rows[hl, :] + causal_bias
        s = _dot_nt(qb[:, sl], kb[:, sl]) * jnp.exp(dm)
        c_h = c_ref[h]
        wi = w_inter[:, hl]
        num = _dot(s.astype(BF16), vb[:, sl]) + wi * _dot(qb[:, sl], c_h.astype(BF16))
        den = (jnp.sum(s, axis=-1, keepdims=True)
               + wi * jnp.sum(qs[:, sl] * n_prev[:, sl], axis=-1, keepdims=True))
        outs.append(num / jnp.maximum(jnp.abs(den), inv_floor[:, hl]))
        kw = k[:, sl] * w_state[:, hl]
        d_h = decay[:, hl]
        c_ref[h] = d_h * c_h + _dot_tn(kw.astype(BF16), vb[:, sl])
        n_new.append(d_h * n_prev[:, sl] + jnp.sum(kw, axis=0, keepdims=True))
        side.step(2)
    n_ref[0:1, :] = jnp.concatenate(n_new, axis=1)
    m_ref[0:1, :] = m_new
    return jnp.concatenate(outs, axis=1)


def _per_head(x, fn):
    return jnp.concatenate([fn(x[:, sl]) for sl in _head_slices()], axis=1)


def _rms_normalize(x):
    return x * lax.rsqrt(jnp.mean(x * x, axis=-1, keepdims=True) + NORM_EPS)


def _layer_normalize(x):
    xc = x - jnp.mean(x, axis=-1, keepdims=True)
    return xc * lax.rsqrt(jnp.mean(xc * xc, axis=-1, keepdims=True) + NORM_EPS)


def _layer_kernel(layer, final_norm,
                  x_ref, ng_ref, fg_ref, w_ref, wb_ref, wo_ref, lruw_ref, mqk_ref, mv_ref, wg_ref, bg_ref,
                  w2_ref, vec_ref, lbl_ref, lvl_ref, cb_ref,
                  o_ref,
                  conv_a, conv_b, lru_h, m_c, m_n, m_m, h_s, g_s):
    @pl.when(pl.program_id(1) == 0)
    def _reset_state():
        conv_a[...] = jnp.zeros(conv_a.shape, F32)
        conv_b[...] = jnp.zeros(conv_b.shape, F32)
        lru_h[...] = jnp.zeros(lru_h.shape, F32)
        m_c[...] = jnp.zeros(m_c.shape, F32)
        m_n[...] = jnp.zeros(m_n.shape, F32)
        m_m[...] = jnp.zeros(m_m.shape, F32)
        h_s[...] = jnp.zeros(h_s.shape, F32)
        g_s[...] = jnp.zeros(g_s.shape, F32)

    def tile(sub, carry):
        rows = pl.ds(pl.multiple_of(sub * TILE, TILE), TILE)
        _mix_tile(layer, final_norm, rows,
                  x_ref, ng_ref, fg_ref, w_ref, wb_ref, wo_ref, lruw_ref, mqk_ref, mv_ref, wg_ref, bg_ref,
                  w2_ref, vec_ref, lbl_ref, lvl_ref, cb_ref, o_ref, conv_a, conv_b, lru_h, m_c, m_n, m_m, h_s, g_s)
        return carry

    lax.fori_loop(0, TILES_PER_STEP, tile, 0)


def _mix_tile(layer, final_norm, rows,
              x_ref, ng_ref, fg_ref, w_ref, wb_ref, wo_ref, lruw_ref, mqk_ref, mv_ref, wg_ref, bg_ref,
              w2_ref, vec_ref, lbl_ref, lvl_ref, cb_ref, o_ref, conv_a, conv_b, lru_h, m_c, m_n, m_m, h_s, g_s):
    half = MXU_DIM

    def vec(row):
        return vec_ref[row:row + 1, :]

    lvl = lvl_ref[...]
    x = x_ref[rows, :]
    hb = (_rms_normalize(x) * ng_ref[...]).astype(BF16)
    side = _SideWork()

    def cat(pieces):
        return jnp.concatenate(pieces, axis=1)

    def queue_in_proj(lo, hi):
        pieces = []
        for c in range(lo, hi, PROJ_CHUNK):
            side.add(lambda c=c: pieces.append(_dot(hb, w_ref[:, c:min(c + PROJ_CHUNK, hi)])))
        return pieces

    def take_in_proj(pieces, lo, hi):
        side.run_until(pieces, -(-(hi - lo) // PROJ_CHUNK))
        return cat(pieces)

    def queue_merge_gate(branch):
        pieces = []
        base = COL_M + branch * D_MODEL
        for c in range(base, base + D_MODEL, PROJ_CHUNK):
            side.add(lambda c=c: pieces.append(_sigmoid(_dot(hb, w_ref[:, c:c + PROJ_CHUNK]))))
        return pieces

    def queue_branch_proj(branch, y):
        out = []
        yb = y.astype(BF16)
        side.add(lambda: out.append(_dot(yb, wb_ref[branch])))
        return out

    u = _dot(hb, w_ref[:, COL_A:COL_B])
    u_b, u_c, u_d = queue_in_proj(COL_B, COL_C), queue_in_proj(COL_C, COL_D), queue_in_proj(COL_D, COL_M)
    gate = [queue_merge_gate(n) for n in range(N_BRANCH)]

    lru_x, lru_z = u[:, :W_BRANCH], u[:, W_BRANCH:]
    xa = _causal_conv(lru_x, conv_a, [vec(ROW_LRU_CONV_W + j) for j in range(CONV_WIDTH)], vec(ROW_LRU_CONV_B))
    xab = xa.astype(BF16)
    g0 = _dot(xab[:, :half], lruw_ref[0])
    g1 = _dot(xab[:, half:], lruw_ref[1])
    side.step(2)
    r = _sigmoid(jnp.concatenate([g0[:, :half], g1[:, :half]], axis=1) + vec(ROW_LRU_BA))
    i = _sigmoid(jnp.concatenate([g0[:, half:], g1[:, half:]], axis=1) + vec(ROW_LRU_BX))
    log_a = (-LRU_C) * r * _softplus(-vec(ROW_LRU_LAMBDA))
    a = jnp.exp(log_a)
    side.step(2)
    h_lru = _linear_recurrence_rows(a, jnp.sqrt(1.0 - a * a) * (i * xa),
                                    lru_h[SUBLANES - 1:SUBLANES, :], side)
    lru_h[...] = h_lru[TILE - SUBLANES:, :]
    y_a = h_lru * _silu(lru_z)

    u = take_in_proj(u_b, COL_B, COL_C)
    proj_0 = queue_branch_proj(0, y_a)
    m_x, m_o, m_z = u[:, :W_BRANCH], u[:, W_BRANCH:2 * W_BRANCH], u[:, 2 * W_BRANCH:]
    xm = _silu(_causal_conv(m_x, conv_b, [vec(ROW_M_CONV_W + j) for j in range(CONV_WIDTH)], vec(ROW_M_CONV_B)))
    xmb = xm.astype(BF16)
    mxb = m_x.astype(BF16)
    qk0 = _dot(xmb[:, :half], mqk_ref[0])
    qk1 = _dot(xmb[:, half:], mqk_ref[1])
    q = jnp.concatenate([qk0[:, :half], qk1[:, :half]], axis=1)
    k = jnp.concatenate([qk0[:, half:], qk1[:, half:]], axis=1)
    v = jnp.concatenate([_dot(mxb[:, :half], mv_ref[0]), _dot(mxb[:, half:], mv_ref[1])], axis=1)
    side.step()
    gates = _dot(jnp.concatenate([q, k, v], axis=1).astype(BF16), wg_ref[...]) + bg_ref[...]
    hm = _mlstm_chunk(q, k, v, gates[:, :LANES], _log_sigmoid(gates[:, LANES:]), m_c, m_n, m_m, cb_ref[...], side)
    hm = _sigmoid(m_o) * hm
    y_b = (_per_head(hm, _layer_normalize) * vec(ROW_M_NORM_W) + vec(ROW_M_SKIP) * xm) * _silu(m_z)

    u = take_in_proj(u_c, COL_C, COL_D)
    proj_1 = queue_branch_proj(1, y_b)
    h_q, h_f = u[:, :W_BRANCH], u[:, W_BRANCH:2 * W_BRANCH]
    h_i, h_z = u[:, 2 * W_BRANCH:3 * W_BRANCH], u[:, 3 * W_BRANCH:]
    e_f = jnp.exp(-jnp.abs(h_f))
    r_f = 1.0 / (1.0 + e_f)
    h_k = jnp.where(h_f >= 0, e_f * r_f, r_f)
    log_fh = jnp.minimum(h_f, 0.0) - jnp.log(1.0 + e_f)
    if layer > 0:
        logits = lbl_ref[...]
        ex = jnp.exp(logits - jnp.max(logits, axis=0, keepdims=True))
        sm = ex / jnp.sum(ex, axis=0, keepdims=True)
        lb = sm[1:2, :]
        for j in range(2, layer + 1):
            lb = lb + sm[j:j + 1, :]
        h_k = (1.0 - lb) * h_k
        lo = jnp.log(lb)
        hi = jnp.log1p(-lb) + log_fh
        log_fh = jnp.maximum(lo, hi) + jnp.log(1.0 + jnp.exp(-jnp.abs(lo - hi)))
    side.step()
    oh = _gla_chunk(_silu(h_q), h_k, h_i, log_fh, h_s, lvl, HEAD_DIM, side)
    y_c = _per_head(oh, _rms_normalize) * vec(ROW_H_NORM_W) * _silu(h_z)

    u = take_in_proj(u_d, COL_D, COL_M)
    proj_2 = queue_branch_proj(2, y_c)
    g_q, g_k, g_v = u[:, :GLA_DK], u[:, GLA_DK:2 * GLA_DK], u[:, 2 * GLA_DK:2 * GLA_DK + W_BRANCH]
    g_lr = u[:, 2 * GLA_DK + W_BRANCH:2 * GLA_DK + W_BRANCH + LANES]
    g_z = u[:, 2 * GLA_DK + W_BRANCH + LANES:]
    log_gk = (_log_sigmoid(_dot(g_lr.astype(BF16), w2_ref[...]) + vec(ROW_G_B_LR2)[:, :GLA_DK])
              * (1.0 / GLA_TAU))
    og = _gla_chunk(g_q, g_k, g_v, log_gk, g_s, lvl, GLA_HEAD_DK, side)
    y_d = _per_head(og, _rms_normalize) * vec(ROW_G_NORM_W) * _silu(g_z)

    side.drain()
    merged = cat(gate[0]) * proj_0[0] + cat(gate[1]) * proj_1[0] + cat(gate[2]) * proj_2[0]
    merged = merged + cat(gate[3]) * _dot(y_d.astype(BF16), wb_ref[3])
    y = x + _dot(merged.astype(BF16), wo_ref[...])
    if final_norm:
        y = _rms_normalize(y) * fg_ref[...]
    o_ref[rows, :] = y


def _block_diag_tiles(w):
    n, b, _ = w.shape
    rows = w.astype(BF16).reshape(n * b // MXU_DIM, MXU_DIM, b)
    col = np.arange(MXU_DIM)
    repeat = (col[None, :] % b == np.arange(b)[:, None]).astype(np.float32)
    repeated = jnp.einsum('trj,jc->trc', rows, jnp.asarray(repeat, BF16), preferred_element_type=F32)
    own_block = col[:, None] // b == col[None, :] // b
    return jnp.where(own_block, repeated, 0.0).astype(BF16)


def _level_table(t):
    idx = np.arange(t)
    xor = idx[:, None] ^ idx[None, :]
    lv = np.floor(np.log2(np.maximum(xor, 1))).astype(np.int32) + 1
    lv = np.where(idx[:, None] == idx[None, :], 0, lv)
    return np.where(idx[:, None] >= idx[None, :], lv, -1).astype(np.float32)


def _causal_bias(t):
    idx = np.arange(t)
    return np.where(idx[:, None] >= idx[None, :], 0.0, -np.inf).astype(np.float32)


def _pack_in_proj_kernel(depth, wt_ref, *refs):
    outs, carry = refs[:depth], refs[depth]
    j = pl.program_id(0)
    lr_block = (COL_LR + LANES) // PACK_COLS - 1
    real_rows = COL_LR + GLA_RANK - lr_block * PACK_COLS
    row = lax.broadcasted_iota(jnp.int32, (PACK_COLS, D_MODEL), 0)

    @pl.when(j == 0)
    def _init_carry():
        carry[...] = jnp.zeros(carry.shape, F32)

    for l in range(depth):
        x = wt_ref[l]
        shifted = jnp.concatenate([carry[l], x[:PACK_COLS - PACK_PAD]], axis=0)
        padded = jnp.where(row < real_rows, x, 0.0)
        y = jnp.where(j > lr_block, shifted, jnp.where(j == lr_block, padded, x))
        outs[l][...] = y.T.astype(BF16)
        carry[l] = x[PACK_COLS - PACK_PAD:]


def _pack_in_proj(w_in):
    depth, _, d_in = w_in.shape
    assert d_in == sum(SPLIT_SIZES) and COL_END == d_in + PACK_PAD
    assert COL_END % PACK_COLS == 0 and (COL_LR + LANES) % PACK_COLS == 0
    last_block = (d_in - 1) // PACK_COLS
    return pl.pallas_call(
        functools.partial(_pack_in_proj_kernel, depth),
        grid=(COL_END // PACK_COLS,),
        in_specs=[pl.BlockSpec((depth, PACK_COLS, D_MODEL), lambda j: (0, jnp.minimum(j, last_block), 0))],
        out_specs=[pl.BlockSpec((D_MODEL, PACK_COLS), lambda j: (0, j))] * depth,
        out_shape=[jax.ShapeDtypeStruct((D_MODEL, COL_END), BF16)] * depth,
        scratch_shapes=[pltpu.VMEM((depth, PACK_PAD, D_MODEL), F32)],
        compiler_params=pltpu.CompilerParams(dimension_semantics=("arbitrary",),
                                             vmem_limit_bytes=VMEM_LIMIT_BYTES),
        name="pack_in_proj",
    )(jnp.swapaxes(w_in, 1, 2))


def _pack_out_proj_kernel(depth, w_br_ref, w_o_ref, *out_refs):
    for l in range(depth):
        out_refs[l][...] = w_br_ref[l].astype(BF16)
        out_refs[depth + l][...] = w_o_ref[l].astype(BF16)


def _pack_out_proj(w_branch, w_out):
    depth = w_branch.shape[0]
    n_steps = D_MODEL // PACK_ROWS
    br_rows = N_BRANCH * W_BRANCH // n_steps
    outs = pl.pallas_call(
        functools.partial(_pack_out_proj_kernel, depth),
        grid=(n_steps,),
        in_specs=[pl.BlockSpec((depth, br_rows, D_MODEL), lambda r: (0, r, 0)),
                  pl.BlockSpec((depth, PACK_ROWS, D_MODEL), lambda r: (0, r, 0))],
        out_specs=([pl.BlockSpec((br_rows, D_MODEL), lambda r: (r, 0))] * depth
                   + [pl.BlockSpec((PACK_ROWS, D_MODEL), lambda r: (r, 0))] * depth),
        out_shape=([jax.ShapeDtypeStruct((N_BRANCH * W_BRANCH, D_MODEL), BF16)] * depth
                   + [jax.ShapeDtypeStruct((D_MODEL, D_MODEL), BF16)] * depth),
        compiler_params=pltpu.CompilerParams(dimension_semantics=("arbitrary",)),
        name="pack_out_proj",
    )(w_branch.reshape(depth, N_BRANCH * W_BRANCH, D_MODEL), w_out)
    return [w.reshape(N_BRANCH, W_BRANCH, D_MODEL) for w in outs[:depth]], outs[depth:]


def _pack_layer(l, lru_conv_w, lru_conv_b, lru_wa, lru_ba, lru_wx, lru_bx, lru_lambda,
                m_conv_w, m_conv_b, m_wq, m_wk, m_wv, m_wi, m_bi, m_wf, m_bf, m_norm_w, m_skip,
                h_norm_w, g_w_lr2, g_b_lr2, g_norm_w):
    lru_w = jnp.concatenate([_block_diag_tiles(lru_wa[l]), _block_diag_tiles(lru_wx[l])], axis=2)
    m_qk = jnp.concatenate([_block_diag_tiles(m_wq[l]), _block_diag_tiles(m_wk[l])], axis=2)
    m_v = _block_diag_tiles(m_wv[l])
    w_gate = jnp.zeros((3 * W_BRANCH, 2 * LANES), F32)
    w_gate = w_gate.at[:, :N_HEADS].set(m_wi[l]).at[:, LANES:LANES + N_HEADS].set(m_wf[l])
    b_gate = jnp.zeros((1, 2 * LANES), F32)
    b_gate = b_gate.at[0, :N_HEADS].set(m_bi[l]).at[0, LANES:LANES + N_HEADS].set(m_bf[l])
    w_lr2 = jnp.pad(g_w_lr2[l], ((0, LANES - GLA_RANK), (0, 0)))
    rows = [lru_conv_w[l][j] for j in range(CONV_WIDTH)]
    rows += [lru_conv_b[l], lru_ba[l], lru_bx[l], lru_lambda[l]]
    rows += [m_conv_w[l][j] for j in range(CONV_WIDTH)]
    rows += [m_conv_b[l], m_norm_w[l], m_skip[l], jnp.tile(h_norm_w[l], N_HEADS), jnp.tile(g_norm_w[l], N_HEADS),
             jnp.pad(g_b_lr2[l], (0, W_BRANCH - GLA_DK))]
    vecs = jnp.stack(rows).astype(F32)
    vecs = jnp.pad(vecs, ((0, N_VEC_ROWS - vecs.shape[0]), (0, 0)))
    return lru_w, m_qk, m_v, w_gate.astype(BF16), b_gate, w_lr2.astype(BF16), vecs


def _layer_call(layer, final_norm, batch, seq):
    block_rows = TILE * TILES_PER_STEP
    n_tiles = seq // block_rows
    tile_spec = pl.BlockSpec((block_rows, D_MODEL), lambda b, j: (b * n_tiles + j, 0))
    resident = pl.BlockSpec(memory_space=pltpu.VMEM)
    state = functools.partial(pltpu.VMEM, dtype=F32)
    return pl.pallas_call(
        functools.partial(_layer_kernel, layer, final_norm),
        grid=(batch, n_tiles),
        in_specs=[tile_spec] + [resident] * 15,
        out_specs=tile_spec,
        out_shape=jax.ShapeDtypeStruct((batch * seq, D_MODEL), F32),
        scratch_shapes=[
            state((SUBLANES, W_BRANCH)),
            state((SUBLANES, W_BRANCH)),
            state((SUBLANES, W_BRANCH)),
            state((N_HEADS, HEAD_DIM, HEAD_DIM)),
            state((SUBLANES, W_BRANCH)),
            state((SUBLANES, LANES)),
            state((N_HEADS, HEAD_DIM, HEAD_DIM)),
            state((N_HEADS, HEAD_DIM, LANES)),
        ],
        compiler_params=pltpu.CompilerParams(
            dimension_semantics=("arbitrary", "arbitrary"),
            vmem_limit_bytes=VMEM_LIMIT_BYTES,
        ),
        name=f"hybrid_layer_{layer}",
    )


def kernel(x, norm_g, w_in, lru_conv_w, lru_conv_b, lru_wa, lru_ba, lru_wx, lru_bx, lru_lambda, m_conv_w, m_conv_b, m_wq, m_wk, m_wv, m_wi, m_bi, m_wf, m_bf, m_norm_w, m_skip, h_lb_logits, h_norm_w, g_w_lr2, g_b_lr2, g_norm_w, w_branch, w_out, final_g):
    batch, seq, d = x.shape
    depth = w_in.shape[0]
    assert d == D_MODEL and seq % (TILE * TILES_PER_STEP) == 0
    lvl = jnp.asarray(_level_table(TILE), BF16)
    causal_bias = jnp.asarray(_causal_bias(TILE))
    lb_logits = h_lb_logits.astype(F32)
    w_all = _pack_in_proj(w_in)
    w_br, w_o = _pack_out_proj(w_branch, w_out)
    y = x.reshape(batch * seq, d).astype(F32)
    for l in range(depth):
        packed = _pack_layer(l, lru_conv_w, lru_conv_b, lru_wa, lru_ba, lru_wx, lru_bx, lru_lambda,
                             m_conv_w, m_conv_b, m_wq, m_wk, m_wv, m_wi, m_bi, m_wf, m_bf, m_norm_w, m_skip,
                             h_norm_w, g_w_lr2, g_b_lr2, g_norm_w)
        (lru_w, m_qk, m_v, w_gate, b_gate, w_lr2, vecs) = packed
        y = _layer_call(l, l == depth - 1, batch, seq)(
            y, norm_g[l].reshape(1, d), final_g.reshape(1, d), w_all[l], w_br[l], w_o[l], lru_w, m_qk, m_v, w_gate,
            b_gate, w_lr2, vecs, lb_logits, lvl, causal_bias)
    return y.reshape(batch, seq, d)
```

```python
import collections
import functools

import numpy as np
import jax
import jax.numpy as jnp
from jax import lax
from jax.experimental import pallas as pl
from jax.experimental.pallas import tpu as pltpu

D_MODEL = 1024
W_BRANCH = 512
N_BRANCH = 4
CONV_WIDTH = 4
LRU_BLOCKS = 8
LRU_C = 8.0
N_HEADS = 4
HEAD_DIM = W_BRANCH // N_HEADS
QKV_BLOCK = 4
GLA_DK = 256
GLA_HEAD_DK = GLA_DK // N_HEADS
GLA_RANK = 16
GLA_TAU = 16.0
NORM_EPS = 1e-6
SPLIT_SIZES = (512, 512, 512, 512, 512, 512, 512, 512, 512, 256, 256, 512, 16, 512, 4096)

LANES = 128
SUBLANES = 8
MXU_DIM = 256
VMEM_LIMIT_BYTES = 56 * 1024 * 1024

TILE = 256
TILES_PER_STEP = 2
PROJ_CHUNK = MXU_DIM

COL_A = 0
COL_B = COL_A + 2 * W_BRANCH
COL_C = COL_B + 3 * W_BRANCH
COL_D = COL_C + 4 * W_BRANCH
COL_LR = COL_D + 2 * GLA_DK + W_BRANCH
COL_M = COL_LR + LANES + W_BRANCH
COL_END = COL_M + N_BRANCH * D_MODEL
PACK_PAD = LANES - GLA_RANK
PACK_COLS = 3 * LANES
PACK_ROWS = 128

ROW_LRU_CONV_W = 0
ROW_LRU_CONV_B = 4
ROW_LRU_BA = 5
ROW_LRU_BX = 6
ROW_LRU_LAMBDA = 7
ROW_M_CONV_W = 8
ROW_M_CONV_B = 12
ROW_M_NORM_W = 13
ROW_M_SKIP = 14
ROW_H_NORM_W = 15
ROW_G_NORM_W = 16
ROW_G_B_LR2 = 17
N_VEC_ROWS = 24

BF16 = jnp.bfloat16
F32 = jnp.float32


class _SideWork:
    def __init__(self):
        self._items = collections.deque()

    def add(self, thunk):
        self._items.append(thunk)

    def step(self, n=1):
        for _ in range(n):
            if self._items:
                self._items.popleft()()

    def run_until(self, pieces, count):
        while len(pieces) < count:
            self._items.popleft()()

    def drain(self):
        while self._items:
            self._items.popleft()()


def _dot(a, b):
    return jnp.dot(a, b, preferred_element_type=F32)


def _dot_nt(a, b):
    return lax.dot_general(a, b, (((1,), (1,)), ((), ())), preferred_element_type=F32)


def _dot_tn(a, b):
    return lax.dot_general(a, b, (((0,), (0,)), ((), ())), preferred_element_type=F32)


def _sigmoid(x):
    return jax.nn.sigmoid(x)


def _silu(x):
    return x * _sigmoid(x)


def _log_sigmoid(x):
    return jnp.minimum(x, 0.0) - jnp.log(1.0 + jnp.exp(-jnp.abs(x)))


def _softplus(x):
    return jnp.maximum(x, 0.0) + jnp.log(1.0 + jnp.exp(-jnp.abs(x)))


def _to_tiles(x):
    t, c = x.shape
    return x.reshape(t // SUBLANES, SUBLANES, c)


def _from_tiles(x3):
    nb, s, c = x3.shape
    return x3.reshape(nb * s, c)


def _sub_iota(shape):
    return lax.broadcasted_iota(jnp.int32, shape, 1)


def _block_ref(x, b):
    t, c = x.shape
    if b == 1:
        x3 = _to_tiles(x)
        return _from_tiles(jnp.where((_sub_iota(x3.shape) & 1) == 1, pltpu.roll(x3, 1, 1), x3))
    if b == 2:
        x3 = _to_tiles(x)
        m = _sub_iota(x3.shape) & 3
        return _from_tiles(jnp.where(m == 0, pltpu.roll(x3, SUBLANES - 1, 1),
                                     jnp.where(m == 1, x3,
                                               jnp.where(m == 2, pltpu.roll(x3, 1, 1), pltpu.roll(x3, 2, 1)))))
    nb = t // (2 * b)
    x3 = x.reshape(nb, 2 * b, c)
    return jnp.broadcast_to(x3[:, b - 1:b, :], (nb, 2 * b, c)).reshape(t, c)


def _prefix_scan_rows(x, combine, identity):
    x3 = _to_tiles(x)
    nb, _, c = x3.shape
    j = _sub_iota(x3.shape)
    for k in (1, 2, 4):
        x3 = combine(jnp.where(j >= k, pltpu.roll(x3, k, 1), identity), x3)
    totals = x3[:, SUBLANES - 1:, :]
    carry = jnp.full((1, 1, c), identity, x.dtype)
    carries = []
    for i in range(nb):
        carries.append(carry)
        carry = combine(carry, totals[i:i + 1])
    return _from_tiles(combine(jnp.concatenate(carries, axis=0), x3))


def _linear_recurrence_rows(a, u, h_prev, side):
    a3, u3 = _to_tiles(a), _to_tiles(u)
    nb, _, c = a3.shape
    j = _sub_iota(a3.shape)
    for k in (1, 2, 4):
        a_back = jnp.where(j >= k, pltpu.roll(a3, k, 1), 1.0)
        u_back = jnp.where(j >= k, pltpu.roll(u3, k, 1), 0.0)
        u3 = a3 * u_back + u3
        a3 = a3 * a_back
        side.step(2)
    a_tot, u_tot = a3[:, SUBLANES - 1:, :], u3[:, SUBLANES - 1:, :]
    h_in = h_prev.reshape(1, 1, c)
    carries = []
    for i in range(nb):
        carries.append(h_in)
        h_in = u_tot[i:i + 1] + a_tot[i:i + 1] * h_in
    return _from_tiles(u3 + a3 * jnp.concatenate(carries, axis=0))


def _causal_conv(x, tail_ref, w, bias):
    t, c = x.shape
    x3 = _to_tiles(x)
    prev3 = jnp.concatenate([tail_ref[...].reshape(1, SUBLANES, c), x3[:-1]], axis=0)
    j = _sub_iota(x3.shape)
    y = bias + w[CONV_WIDTH - 1] * x
    for back in range(1, CONV_WIDTH):
        shifted = pltpu.roll(jnp.where(j >= SUBLANES - back, prev3, x3), back, 1)
        y = y + w[CONV_WIDTH - 1 - back] * _from_tiles(shifted)
    tail_ref[...] = x[t - SUBLANES:, :]
    return y


def _head_slices():
    return [slice(h * HEAD_DIM, (h + 1) * HEAD_DIM) for h in range(N_HEADS)]


def _gla_chunk(q, k, v, g, st_ref, lvl, head_dk, side):
    t = q.shape[0]
    per_tile = LANES // head_dk
    lane = lax.broadcasted_iota(jnp.int32, (1, LANES), 1)

    def key_tile(x, h):
        tile = h // per_tile
        return x[:, tile * LANES:(tile + 1) * LANES]

    def query_tile(x, h):
        if per_tile == 1:
            return key_tile(x, h)
        own = ((lane // head_dk) == (h % per_tile)).astype(BF16)
        return key_tile(x, h) * own

    q = q * (head_dk ** -0.5)
    vb = v.astype(BF16)
    qb = q.astype(BF16)
    kb = k.astype(BF16)
    heads = range(N_HEADS)
    zero = jnp.zeros((), BF16)

    def diagonal(h):
        prod = key_tile(q, h) * key_tile(k, h)
        if per_tile > 1:
            prod = jnp.where((lane // head_dk) == (h % per_tile), prod, 0.0)
        return jnp.sum(prod, axis=-1, keepdims=True)

    half_t = t // 2
    top, bottom = slice(0, half_t), slice(half_t, t)

    def diag_quadrants(full):
        return full[top, top], full[bottom, bottom]

    def diag_quadrant_scores(h):
        d = diagonal(h)
        return [jnp.where(lvl == 0, jnp.broadcast_to(d[rows], (half_t, half_t)).astype(BF16), zero)
                for rows in (top, bottom)]

    scores = [diag_quadrant_scores(h) for h in heads]
    lower_left = [None] * N_HEADS
    g_cum = _prefix_scan_rows(g, jnp.add, 0.0)
    b, level = 1, 1
    while b < t:
        eb = jnp.exp(-jnp.abs(g_cum - _block_ref(g_cum, b))).astype(BF16)
        qe = qb * eb
        ke = kb * eb
        for h in heads:
            if b == half_t:
                lower_left[h] = _dot_nt(query_tile(qe, h)[bottom], key_tile(ke, h)[top]).astype(BF16)
            else:
                full = _dot_nt(query_tile(qe, h), key_tile(ke, h))
                scores[h] = [jnp.where(lvl == level, a.astype(BF16), s)
                             for a, s in zip(diag_quadrants(full), scores[h])]
        side.step()
        b *= 2
        level += 1
    zeros_q = jnp.zeros((half_t, half_t), BF16)
    scores = [jnp.concatenate([jnp.concatenate([scores[h][0], zeros_q], axis=1),
                               jnp.concatenate([lower_left[h], scores[h][1]], axis=1)], axis=0) for h in heads]
    g_last = g_cum[t - 1:t, :]
    q_in = (q * jnp.exp(g_cum)).astype(BF16)
    k_out = (k * jnp.exp(g_last - g_cum)).astype(BF16)
    state_decay = jnp.exp(g_last)
    outs = []
    for h in heads:
        sl = slice(h * HEAD_DIM, (h + 1) * HEAD_DIM)
        st = st_ref[h]
        outs.append(_dot(scores[h], vb[:, sl]) + _dot_nt(query_tile(q_in, h), st.astype(BF16)))
        st_ref[h] = st * key_tile(state_decay, h) + _dot_tn(vb[:, sl], key_tile(k_out, h))
        side.step()
    return jnp.concatenate(outs, axis=1)


def _mlstm_chunk(q, k, v, log_i, log_f, c_ref, n_ref, m_ref, causal_bias, side):
    t = q.shape[0]
    heads = _head_slices()
    b = _prefix_scan_rows(log_f, jnp.add, 0.0)
    z = log_i - b
    zmax = _prefix_scan_rows(z, jnp.maximum, -jnp.inf)
    m_prev = m_ref[0:1, :]
    m_t = b + jnp.maximum(m_prev, zmax)
    col = b - m_t
    w_inter = jnp.exp(b + m_prev - m_t)
    inv_floor = jnp.exp(-m_t)
    z_rows = z.T
    b_last = b[t - 1:t, :]
    m_new = m_t[t - 1:t, :]
    w_state = jnp.exp(z + b_last - m_new)
    decay = jnp.exp(b_last + m_prev - m_new)
    qs = q * (HEAD_DIM ** -0.5)
    qb = qs.astype(BF16)
    kb = k.astype(BF16)
    vb = v.astype(BF16)
    n_prev = n_ref[0:1, :]
    outs, n_new = [], []
    for h, sl in enumerate(heads):
        hl = slice(h, h + 1)
        dm = col[:, hl] + z_rows[hl, :] + causal_bias
        s = _dot_nt(qb[:, sl], kb[:, sl]) * jnp.exp(dm)
        c_h = c_ref[h]
        wi = w_inter[:, hl]
        num = _dot(s.astype(BF16), vb[:, sl]) + wi * _dot(qb[:, sl], c_h.astype(BF16))
        den = (jnp.sum(s, axis=-1, keepdims=True)
               + wi * jnp.sum(qs[:, sl] * n_prev[:, sl], axis=-1, keepdims=True))
        outs.append(num / jnp.maximum(jnp.abs(den), inv_floor[:, hl]))
        kw = k[:, sl] * w_state[:, hl]
        d_h = decay[:, hl]
        c_ref[h] = d_h * c_h + _dot_tn(kw.astype(BF16), vb[:, sl])
        n_new.append(d_h * n_prev[:, sl] + jnp.sum(kw, axis=0, keepdims=True))
        side.step(2)
    n_ref[0:1, :] = jnp.concatenate(n_new, axis=1)
    m_ref[0:1, :] = m_new
    return jnp.concatenate(outs, axis=1)


def _per_head(x, fn):
    return jnp.concatenate([fn(x[:, sl]) for sl in _head_slices()], axis=1)


def _rms_normalize(x):
    return x * lax.rsqrt(jnp.mean(x * x, axis=-1, keepdims=True) + NORM_EPS)


def _layer_normalize(x):
    xc = x - jnp.mean(x, axis=-1, keepdims=True)
    return xc * lax.rsqrt(jnp.mean(xc * xc, axis=-1, keepdims=True) + NORM_EPS)


def _layer_kernel(layer, final_norm,
                  x_ref, ng_ref, fg_ref, w_ref, wb_ref, wo_ref, lruw_ref, mqk_ref, mv_ref, wg_ref, bg_ref,
                  w2_ref, vec_ref, lbl_ref, lvl_ref, cb_ref,
                  o_ref,
                  conv_a, conv_b, lru_h, m_c, m_n, m_m, h_s, g_s):
    @pl.when(pl.program_id(1) == 0)
    def _reset_state():
        conv_a[...] = jnp.zeros(conv_a.shape, F32)
        conv_b[...] = jnp.zeros(conv_b.shape, F32)
        lru_h[...] = jnp.zeros(lru_h.shape, F32)
        m_c[...] = jnp.zeros(m_c.shape, F32)
        m_n[...] = jnp.zeros(m_n.shape, F32)
        m_m[...] = jnp.zeros(m_m.shape, F32)
        h_s[...] = jnp.zeros(h_s.shape, F32)
        g_s[...] = jnp.zeros(g_s.shape, F32)

    def tile(sub, carry):
        rows = pl.ds(pl.multiple_of(sub * TILE, TILE), TILE)
        _mix_tile(layer, final_norm, rows,
                  x_ref, ng_ref, fg_ref, w_ref, wb_ref, wo_ref, lruw_ref, mqk_ref, mv_ref, wg_ref, bg_ref,
                  w2_ref, vec_ref, lbl_ref, lvl_ref, cb_ref, o_ref, conv_a, conv_b, lru_h, m_c, m_n, m_m, h_s, g_s)
        return carry

    lax.fori_loop(0, TILES_PER_STEP, tile, 0)


def _mix_tile(layer, final_norm, rows,
              x_ref, ng_ref, fg_ref, w_ref, wb_ref, wo_ref, lruw_ref, mqk_ref, mv_ref, wg_ref, bg_ref,
              w2_ref, vec_ref, lbl_ref, lvl_ref, cb_ref, o_ref, conv_a, conv_b, lru_h, m_c, m_n, m_m, h_s, g_s):
    half = MXU_DIM

    def vec(row):
        return vec_ref[row:row + 1, :]

    lvl = lvl_ref[...]
    x = x_ref[rows, :]
    hb = (_rms_normalize(x) * ng_ref[...]).astype(BF16)
    side = _SideWork()

    def cat(pieces):
        return jnp.concatenate(pieces, axis=1)

    def queue_in_proj(lo, hi):
        pieces = []
        for c in range(lo, hi, PROJ_CHUNK):
            side.add(lambda c=c: pieces.append(_dot(hb, w_ref[:, c:min(c + PROJ_CHUNK, hi)])))
        return pieces

    def take_in_proj(pieces, lo, hi):
        side.run_until(pieces, -(-(hi - lo) // PROJ_CHUNK))
        return cat(pieces)

    def queue_merge_gate(branch):
        pieces = []
        base = COL_M + branch * D_MODEL
        for c in range(base, base + D_MODEL, PROJ_CHUNK):
            side.add(lambda c=c: pieces.append(_sigmoid(_dot(hb, w_ref[:, c:c + PROJ_CHUNK]))))
        return pieces

    def queue_branch_proj(branch, y):
        out = []
        yb = y.astype(BF16)
        side.add(lambda: out.append(_dot(yb, wb_ref[branch])))
        return out

    u = _dot(hb, w_ref[:, COL_A:COL_B])
    u_b, u_c, u_d = queue_in_proj(COL_B, COL_C), queue_in_proj(COL_C, COL_D), queue_in_proj(COL_D, COL_M)
    gate = [queue_merge_gate(n) for n in range(N_BRANCH)]

    lru_x, lru_z = u[:, :W_BRANCH], u[:, W_BRANCH:]
    xa = _causal_conv(lru_x, conv_a, [vec(ROW_LRU_CONV_W + j) for j in range(CONV_WIDTH)], vec(ROW_LRU_CONV_B))
    xab = xa.astype(BF16)
    g0 = _dot(xab[:, :half], lruw_ref[0])
    g1 = _dot(xab[:, half:], lruw_ref[1])
    side.step(2)
    r = _sigmoid(jnp.concatenate([g0[:, :half], g1[:, :half]], axis=1) + vec(ROW_LRU_BA))
    i = _sigmoid(jnp.concatenate([g0[:, half:], g1[:, half:]], axis=1) + vec(ROW_LRU_BX))
    log_a = (-LRU_C) * r * _softplus(-vec(ROW_LRU_LAMBDA))
    a = jnp.exp(log_a)
    side.step(2)
    h_lru = _linear_recurrence_rows(a, jnp.sqrt(1.0 - a * a) * (i * xa),
                                    lru_h[SUBLANES - 1:SUBLANES, :], side)
    lru_h[...] = h_lru[TILE - SUBLANES:, :]
    y_a = h_lru * _silu(lru_z)

    u = take_in_proj(u_b, COL_B, COL_C)
    proj_0 = queue_branch_proj(0, y_a)
    m_x, m_o, m_z = u[:, :W_BRANCH], u[:, W_BRANCH:2 * W_BRANCH], u[:, 2 * W_BRANCH:]
    xm = _silu(_causal_conv(m_x, conv_b, [vec(ROW_M_CONV_W + j) for j in range(CONV_WIDTH)], vec(ROW_M_CONV_B)))
    xmb = xm.astype(BF16)
    mxb = m_x.astype(BF16)
    qk0 = _dot(xmb[:, :half], mqk_ref[0])
    qk1 = _dot(xmb[:, half:], mqk_ref[1])
    q = jnp.concatenate([qk0[:, :half], qk1[:, :half]], axis=1)
    k = jnp.concatenate([qk0[:, half:], qk1[:, half:]], axis=1)
    v = jnp.concatenate([_dot(mxb[:, :half], mv_ref[0]), _dot(mxb[:, half:], mv_ref[1])], axis=1)
    side.step()
    gates = _dot(jnp.concatenate([q, k, v], axis=1).astype(BF16), wg_ref[...]) + bg_ref[...]
    hm = _mlstm_chunk(q, k, v, gates[:, :LANES], _log_sigmoid(gates[:, LANES:]), m_c, m_n, m_m, cb_ref[...], side)
    hm = _sigmoid(m_o) * hm
    y_b = (_per_head(hm, _layer_normalize) * vec(ROW_M_NORM_W) + vec(ROW_M_SKIP) * xm) * _silu(m_z)

    u = take_in_proj(u_c, COL_C, COL_D)
    proj_1 = queue_branch_proj(1, y_b)
    h_q, h_f = u[:, :W_BRANCH], u[:, W_BRANCH:2 * W_BRANCH]
    h_i, h_z = u[:, 2 * W_BRANCH:3 * W_BRANCH], u[:, 3 * W_BRANCH:]
    e_f = jnp.exp(-jnp.abs(h_f))
    r_f = 1.0 / (1.0 + e_f)
    h_k = jnp.where(h_f >= 0, e_f * r_f, r_f)
    log_fh = jnp.minimum(h_f, 0.0) - jnp.log(1.0 + e_f)
    if layer > 0:
        logits = lbl_ref[...]
        ex = jnp.exp(logits - jnp.max(logits, axis=0, keepdims=True))
        sm = ex / jnp.sum(ex, axis=0, keepdims=True)
        lb = sm[1:2, :]
        for j in range(2, layer + 1):
            lb = lb + sm[j:j + 1, :]
        h_k = (1.0 - lb) * h_k
        lo = jnp.log(lb)
        hi = jnp.log1p(-lb) + log_fh
        log_fh = jnp.maximum(lo, hi) + jnp.log(1.0 + jnp.exp(-jnp.abs(lo - hi)))
    side.step()
    oh = _gla_chunk(_silu(h_q), h_k, h_i, log_fh, h_s, lvl, HEAD_DIM, side)
    y_c = _per_head(oh, _rms_normalize) * vec(ROW_H_NORM_W) * _silu(h_z)

    u = take_in_proj(u_d, COL_D, COL_M)
    proj_2 = queue_branch_proj(2, y_c)
    g_q, g_k, g_v = u[:, :GLA_DK], u[:, GLA_DK:2 * GLA_DK], u[:, 2 * GLA_DK:2 * GLA_DK + W_BRANCH]
    g_lr = u[:, 2 * GLA_DK + W_BRANCH:2 * GLA_DK + W_BRANCH + LANES]
    g_z = u[:, 2 * GLA_DK + W_BRANCH + LANES:]
    log_gk = (_log_sigmoid(_dot(g_lr.astype(BF16), w2_ref[...]) + vec(ROW_G_B_LR2)[:, :GLA_DK])
              * (1.0 / GLA_TAU))
    og = _gla_chunk(g_q, g_k, g_v, log_gk, g_s, lvl, GLA_HEAD_DK, side)
    y_d = _per_head(og, _rms_normalize) * vec(ROW_G_NORM_W) * _silu(g_z)

    side.drain()
    merged = cat(gate[0]) * proj_0[0] + cat(gate[1]) * proj_1[0] + cat(gate[2]) * proj_2[0]
    merged = merged + cat(gate[3]) * _dot(y_d.astype(BF16), wb_ref[3])
    y = x + _dot(merged.astype(BF16), wo_ref[...])
    if final_norm:
        y = _rms_normalize(y) * fg_ref[...]
    o_ref[rows, :] = y


def _block_diag_tiles(w):
    n, b, _ = w.shape
    rows = w.astype(BF16).reshape(n * b // MXU_DIM, MXU_DIM, b)
    col = np.arange(MXU_DIM)
    repeat = (col[None, :] % b == np.arange(b)[:, None]).astype(np.float32)
    repeated = jnp.einsum('trj,jc->trc', rows, jnp.asarray(repeat, BF16), preferred_element_type=F32)
    own_block = col[:, None] // b == col[None, :] // b
    return jnp.where(own_block, repeated, 0.0).astype(BF16)


def _level_table(t):
    idx = np.arange(t)
    xor = idx[:, None] ^ idx[None, :]
    lv = np.floor(np.log2(np.maximum(xor, 1))).astype(np.int32) + 1
    lv = np.where(idx[:, None] == idx[None, :], 0, lv)
    return np.where(idx[:, None] >= idx[None, :], lv, -1).astype(np.float32)


def _causal_bias(t):
    idx = np.arange(t)
    return np.where(idx[:, None] >= idx[None, :], 0.0, -np.inf).astype(np.float32)


def _pack_in_proj_kernel(depth, wt_ref, *refs):
    outs, carry = refs[:depth], refs[depth]
    j = pl.program_id(0)
    lr_block = (COL_LR + LANES) // PACK_COLS - 1
    real_rows = COL_LR + GLA_RANK - lr_block * PACK_COLS
    row = lax.broadcasted_iota(jnp.int32, (PACK_COLS, D_MODEL), 0)

    @pl.when(j == 0)
    def _init_carry():
        carry[...] = jnp.zeros(carry.shape, F32)

    for l in range(depth):
        x = wt_ref[l]
        shifted = jnp.concatenate([carry[l], x[:PACK_COLS - PACK_PAD]], axis=0)
        padded = jnp.where(row < real_rows, x, 0.0)
        y = jnp.where(j > lr_block, shifted, jnp.where(j == lr_block, padded, x))
        outs[l][...] = y.T.astype(BF16)
        carry[l] = x[PACK_COLS - PACK_PAD:]


def _pack_in_proj(w_in):
    depth, _, d_in = w_in.shape
    assert d_in == sum(SPLIT_SIZES) and COL_END == d_in + PACK_PAD
    assert COL_END % PACK_COLS == 0 and (COL_LR + LANES) % PACK_COLS == 0
    last_block = (d_in - 1) // PACK_COLS
    return pl.pallas_call(
        functools.partial(_pack_in_proj_kernel, depth),
        grid=(COL_END // PACK_COLS,),
        in_specs=[pl.BlockSpec((depth, PACK_COLS, D_MODEL), lambda j: (0, jnp.minimum(j, last_block), 0))],
        out_specs=[pl.BlockSpec((D_MODEL, PACK_COLS), lambda j: (0, j))] * depth,
        out_shape=[jax.ShapeDtypeStruct((D_MODEL, COL_END), BF16)] * depth,
        scratch_shapes=[pltpu.VMEM((depth, PACK_PAD, D_MODEL), F32)],
        compiler_params=pltpu.CompilerParams(dimension_semantics=("arbitrary",),
                                             vmem_limit_bytes=VMEM_LIMIT_BYTES),
        name="pack_in_proj",
    )(jnp.swapaxes(w_in, 1, 2))


def _pack_out_proj_kernel(depth, w_br_ref, w_o_ref, *out_refs):
    for l in range(depth):
        out_refs[l][...] = w_br_ref[l].astype(BF16)
        out_refs[depth + l][...] = w_o_ref[l].astype(BF16)


def _pack_out_proj(w_branch, w_out):
    depth = w_branch.shape[0]
    n_steps = D_MODEL // PACK_ROWS
    br_rows = N_BRANCH * W_BRANCH // n_steps
    outs = pl.pallas_call(
        functools.partial(_pack_out_proj_kernel, depth),
        grid=(n_steps,),
        in_specs=[pl.BlockSpec((depth, br_rows, D_MODEL), lambda r: (0, r, 0)),
                  pl.BlockSpec((depth, PACK_ROWS, D_MODEL), lambda r: (0, r, 0))],
        out_specs=([pl.BlockSpec((br_rows, D_MODEL), lambda r: (r, 0))] * depth
                   + [pl.BlockSpec((PACK_ROWS, D_MODEL), lambda r: (r, 0))] * depth),
        out_shape=([jax.ShapeDtypeStruct((N_BRANCH * W_BRANCH, D_MODEL), BF16)] * depth
                   + [jax.ShapeDtypeStruct((D_MODEL, D_MODEL), BF16)] * depth),
        compiler_params=pltpu.CompilerParams(dimension_semantics=("arbitrary",)),
        name="pack_out_proj",
    )(w_branch.reshape(depth, N_BRANCH * W_BRANCH, D_MODEL), w_out)
    return [w.reshape(N_BRANCH, W_BRANCH, D_MODEL) for w in outs[:depth]], outs[depth:]


def _pack_layer(l, lru_conv_w, lru_conv_b, lru_wa, lru_ba, lru_wx, lru_bx, lru_lambda,
                m_conv_w, m_conv_b, m_wq, m_wk, m_wv, m_wi, m_bi, m_wf, m_bf, m_norm_w, m_skip,
                h_norm_w, g_w_lr2, g_b_lr2, g_norm_w):
    lru_w = jnp.concatenate([_block_diag_tiles(lru_wa[l]), _block_diag_tiles(lru_wx[l])], axis=2)
    m_qk = jnp.concatenate([_block_diag_tiles(m_wq[l]), _block_diag_tiles(m_wk[l])], axis=2)
    m_v = _block_diag_tiles(m_wv[l])
    w_gate = jnp.zeros((3 * W_BRANCH, 2 * LANES), F32)
    w_gate = w_gate.at[:, :N_HEADS].set(m_wi[l]).at[:, LANES:LANES + N_HEADS].set(m_wf[l])
    b_gate = jnp.zeros((1, 2 * LANES), F32)
    b_gate = b_gate.at[0, :N_HEADS].set(m_bi[l]).at[0, LANES:LANES + N_HEADS].set(m_bf[l])
    w_lr2 = jnp.pad(g_w_lr2[l], ((0, LANES - GLA_RANK), (0, 0)))
    rows = [lru_conv_w[l][j] for j in range(CONV_WIDTH)]
    rows += [lru_conv_b[l], lru_ba[l], lru_bx[l], lru_lambda[l]]
    rows += [m_conv_w[l][j] for j in range(CONV_WIDTH)]
    rows += [m_conv_b[l], m_norm_w[l], m_skip[l], jnp.tile(h_norm_w[l], N_HEADS), jnp.tile(g_norm_w[l], N_HEADS),
             jnp.pad(g_b_lr2[l], (0, W_BRANCH - GLA_DK))]
    vecs = jnp.stack(rows).astype(F32)
    vecs = jnp.pad(vecs, ((0, N_VEC_ROWS - vecs.shape[0]), (0, 0)))
    return lru_w, m_qk, m_v, w_gate.astype(BF16), b_gate, w_lr2.astype(BF16), vecs


def _layer_call(layer, final_norm, batch, seq):
    block_rows = TILE * TILES_PER_STEP
    n_tiles = seq // block_rows
    tile_spec = pl.BlockSpec((block_rows, D_MODEL), lambda b, j: (b * n_tiles + j, 0))
    resident = pl.BlockSpec(memory_space=pltpu.VMEM)
    state = functools.partial(pltpu.VMEM, dtype=F32)
    return pl.pallas_call(
        functools.partial(_layer_kernel, layer, final_norm),
        grid=(batch, n_tiles),
        in_specs=[tile_spec] + [resident] * 15,
        out_specs=tile_spec,
        out_shape=jax.ShapeDtypeStruct((batch * seq, D_MODEL), F32),
        scratch_shapes=[
            state((SUBLANES, W_BRANCH)),
            state((SUBLANES, W_BRANCH)),
            state((SUBLANES, W_BRANCH)),
            state((N_HEADS, HEAD_DIM, HEAD_DIM)),
            state((SUBLANES, W_BRANCH)),
            state((SUBLANES, LANES)),
            state((N_HEADS, HEAD_DIM, HEAD_DIM)),
            state((N_HEADS, HEAD_DIM, LANES)),
        ],
        compiler_params=pltpu.CompilerParams(
            dimension_semantics=("arbitrary", "arbitrary"),
            vmem_limit_bytes=VMEM_LIMIT_BYTES,
        ),
        name=f"hybrid_layer_{layer}",
    )


def kernel(x, norm_g, w_in, lru_conv_w, lru_conv_b, lru_wa, lru_ba, lru_wx, lru_bx, lru_lambda, m_conv_w, m_conv_b, m_wq, m_wk, m_wv, m_wi, m_bi, m_wf, m_bf, m_norm_w, m_skip, h_lb_logits, h_norm_w, g_w_lr2, g_b_lr2, g_norm_w, w_branch, w_out, final_g):
    batch, seq, d = x.shape
    depth = w_in.shape[0]
    assert d == D_MODEL and seq % (TILE * TILES_PER_STEP) == 0
    lvl = jnp.asarray(_level_table(TILE // 2), BF16)
    causal_bias = jnp.asarray(_causal_bias(TILE))
    lb_logits = h_lb_logits.astype(F32)
    w_all = _pack_in_proj(w_in)
    w_br, w_o = _pack_out_proj(w_branch, w_out)
    y = x.reshape(batch * seq, d).astype(F32)
    for l in range(depth):
        packed = _pack_layer(l, lru_conv_w, lru_conv_b, lru_wa, lru_ba, lru_wx, lru_bx, lru_lambda,
                             m_conv_w, m_conv_b, m_wq, m_wk, m_wv, m_wi, m_bi, m_wf, m_bf, m_norm_w, m_skip,
                             h_norm_w, g_w_lr2, g_b_lr2, g_norm_w)
        (lru_w, m_qk, m_v, w_gate, b_gate, w_lr2, vecs) = packed
        y = _layer_call(l, l == depth - 1, batch, seq)(
            y, norm_g[l].reshape(1, d), final_g.reshape(1, d), w_all[l], w_br[l], w_o[l], lru_w, m_qk, m_v, w_gate,
            b_gate, w_lr2, vecs, lb_logits, lvl, causal_bias)
    return y.reshape(batch, seq, d)
```

```python
import collections
import functools

import numpy as np
import jax
import jax.numpy as jnp
from jax import lax
from jax.experimental import pallas as pl
from jax.experimental.pallas import tpu as pltpu

D_MODEL = 1024
W_BRANCH = 512
N_BRANCH = 4
CONV_WIDTH = 4
LRU_BLOCKS = 8
LRU_C = 8.0
N_HEADS = 4
HEAD_DIM = W_BRANCH // N_HEADS
QKV_BLOCK = 4
GLA_DK = 256
GLA_HEAD_DK = GLA_DK // N_HEADS
GLA_RANK = 16
GLA_TAU = 16.0
NORM_EPS = 1e-6
SPLIT_SIZES = (512, 512, 512, 512, 512, 512, 512, 512, 512, 256, 256, 512, 16, 512, 4096)

LANES = 128
SUBLANES = 8
MXU_DIM = 256
VMEM_LIMIT_BYTES = 56 * 1024 * 1024

TILE = 256
TILES_PER_STEP = 2
PROJ_CHUNK = MXU_DIM

COL_A = 0
COL_B = COL_A + 2 * W_BRANCH
COL_C = COL_B + 3 * W_BRANCH
COL_D = COL_C + 4 * W_BRANCH
COL_LR = COL_D + 2 * GLA_DK + W_BRANCH
COL_M = COL_LR + LANES + W_BRANCH
COL_END = COL_M + N_BRANCH * D_MODEL
PACK_PAD = LANES - GLA_RANK
PACK_COLS = 3 * LANES
PACK_ROWS = 128

ROW_LRU_CONV_W = 0
ROW_LRU_CONV_B = 4
ROW_LRU_BA = 5
ROW_LRU_BX = 6
ROW_LRU_LAMBDA = 7
ROW_M_CONV_W = 8
ROW_M_CONV_B = 12
ROW_M_NORM_W = 13
ROW_M_SKIP = 14
ROW_H_NORM_W = 15
ROW_G_NORM_W = 16
ROW_G_B_LR2 = 17
N_VEC_ROWS = 24

BF16 = jnp.bfloat16
F32 = jnp.float32


class _SideWork:
    def __init__(self):
        self._items = collections.deque()

    def add(self, thunk):
        self._items.append(thunk)

    def step(self, n=1):
        for _ in range(n):
            if self._items:
                self._items.popleft()()

    def run_until(self, pieces, count):
        while len(pieces) < count:
            self._items.popleft()()

    def drain(self):
        while self._items:
            self._items.popleft()()


def _dot(a, b):
    return jnp.dot(a, b, preferred_element_type=F32)


def _dot_nt(a, b):
    return lax.dot_general(a, b, (((1,), (1,)), ((), ())), preferred_element_type=F32)


def _dot_tn(a, b):
    return lax.dot_general(a, b, (((0,), (0,)), ((), ())), preferred_element_type=F32)


def _sigmoid(x):
    return jax.nn.sigmoid(x)


def _silu(x):
    return x * _sigmoid(x)


def _log_sigmoid(x):
    return jnp.minimum(x, 0.0) - jnp.log(1.0 + jnp.exp(-jnp.abs(x)))


def _softplus(x):
    return jnp.maximum(x, 0.0) + jnp.log(1.0 + jnp.exp(-jnp.abs(x)))


def _to_tiles(x):
    t, c = x.shape
    return x.reshape(t // SUBLANES, SUBLANES, c)


def _from_tiles(x3):
    nb, s, c = x3.shape
    return x3.reshape(nb * s, c)


def _sub_iota(shape):
    return lax.broadcasted_iota(jnp.int32, shape, 1)


def _block_ref(x, b):
    t, c = x.shape
    if b == 1:
        x3 = _to_tiles(x)
        return _from_tiles(jnp.where((_sub_iota(x3.shape) & 1) == 1, pltpu.roll(x3, 1, 1), x3))
    if b == 2:
        x3 = _to_tiles(x)
        m = _sub_iota(x3.shape) & 3
        return _from_tiles(jnp.where(m == 0, pltpu.roll(x3, SUBLANES - 1, 1),
                                     jnp.where(m == 1, x3,
                                               jnp.where(m == 2, pltpu.roll(x3, 1, 1), pltpu.roll(x3, 2, 1)))))
    nb = t // (2 * b)
    x3 = x.reshape(nb, 2 * b, c)
    return jnp.broadcast_to(x3[:, b - 1:b, :], (nb, 2 * b, c)).reshape(t, c)


def _prefix_scan_rows(x, combine, identity):
    x3 = _to_tiles(x)
    nb, _, c = x3.shape
    j = _sub_iota(x3.shape)
    for k in (1, 2, 4):
        x3 = combine(jnp.where(j >= k, pltpu.roll(x3, k, 1), identity), x3)
    totals = x3[:, SUBLANES - 1:, :]
    carry = jnp.full((1, 1, c), identity, x.dtype)
    carries = []
    for i in range(nb):
        carries.append(carry)
        carry = combine(carry, totals[i:i + 1])
    return _from_tiles(combine(jnp.concatenate(carries, axis=0), x3))


def _linear_recurrence_rows(a, u, h_prev, side):
    a3, u3 = _to_tiles(a), _to_tiles(u)
    nb, _, c = a3.shape
    j = _sub_iota(a3.shape)
    for k in (1, 2, 4):
        a_back = jnp.where(j >= k, pltpu.roll(a3, k, 1), 1.0)
        u_back = jnp.where(j >= k, pltpu.roll(u3, k, 1), 0.0)
        u3 = a3 * u_back + u3
        a3 = a3 * a_back
        side.step(2)
    a_tot, u_tot = a3[:, SUBLANES - 1:, :], u3[:, SUBLANES - 1:, :]
    h_in = h_prev.reshape(1, 1, c)
    carries = []
    for i in range(nb):
        carries.append(h_in)
        h_in = u_tot[i:i + 1] + a_tot[i:i + 1] * h_in
    return _from_tiles(u3 + a3 * jnp.concatenate(carries, axis=0))


def _causal_conv(x, tail_ref, w, bias):
    t, c = x.shape
    x3 = _to_tiles(x)
    prev3 = jnp.concatenate([tail_ref[...].reshape(1, SUBLANES, c), x3[:-1]], axis=0)
    j = _sub_iota(x3.shape)
    y = bias + w[CONV_WIDTH - 1] * x
    for back in range(1, CONV_WIDTH):
        shifted = pltpu.roll(jnp.where(j >= SUBLANES - back, prev3, x3), back, 1)
        y = y + w[CONV_WIDTH - 1 - back] * _from_tiles(shifted)
    tail_ref[...] = x[t - SUBLANES:, :]
    return y


def _head_slices():
    return [slice(h * HEAD_DIM, (h + 1) * HEAD_DIM) for h in range(N_HEADS)]


def _gla_chunk(q, k, v, g, st_ref, lvl, head_dk, side):
    t = q.shape[0]
    per_tile = LANES // head_dk
    lane = lax.broadcasted_iota(jnp.int32, (1, LANES), 1)

    def key_tile(x, h):
        tile = h // per_tile
        return x[:, tile * LANES:(tile + 1) * LANES]

    def query_tile(x, h):
        if per_tile == 1:
            return key_tile(x, h)
        own = ((lane // head_dk) == (h % per_tile)).astype(BF16)
        return key_tile(x, h) * own

    q = q * (head_dk ** -0.5)
    vb = v.astype(BF16)
    qb = q.astype(BF16)
    kb = k.astype(BF16)
    heads = range(N_HEADS)
    zero = jnp.zeros((), BF16)

    def diagonal(h):
        prod = key_tile(q, h) * key_tile(k, h)
        if per_tile > 1:
            prod = jnp.where((lane // head_dk) == (h % per_tile), prod, 0.0)
        return jnp.sum(prod, axis=-1, keepdims=True)

    half_t = t // 2
    top, bottom = slice(0, half_t), slice(half_t, t)

    def diag_quadrants(full):
        return full[top, top], full[bottom, bottom]

    def diag_quadrant_scores(h):
        d = diagonal(h)
        return [jnp.where(lvl == 0, jnp.broadcast_to(d[rows], (half_t, half_t)).astype(BF16), zero)
                for rows in (top, bottom)]

    scores = [diag_quadrant_scores(h) for h in heads]
    lower_left = [None] * N_HEADS
    g_cum = _prefix_scan_rows(g, jnp.add, 0.0)
    b, level = 1, 1
    while b < t:
        eb = jnp.exp(-jnp.abs(g_cum - _block_ref(g_cum, b))).astype(BF16)
        qe = qb * eb
        ke = kb * eb
        for h in heads:
            if b == half_t:
                lower_left[h] = _dot_nt(query_tile(qe, h)[bottom], key_tile(ke, h)[top]).astype(BF16)
            else:
                full = _dot_nt(query_tile(qe, h), key_tile(ke, h))
                scores[h] = [jnp.where(lvl == level, a.astype(BF16), s)
                             for a, s in zip(diag_quadrants(full), scores[h])]
        side.step()
        b *= 2
        level += 1
    zeros_q = jnp.zeros((half_t, half_t), BF16)
    scores = [jnp.concatenate([jnp.concatenate([scores[h][0], zeros_q], axis=1),
                               jnp.concatenate([lower_left[h], scores[h][1]], axis=1)], axis=0) for h in heads]
    g_last = g_cum[t - 1:t, :]
    q_in = (q * jnp.exp(g_cum)).astype(BF16)
    k_out = (k * jnp.exp(g_last - g_cum)).astype(BF16)
    state_decay = jnp.exp(g_last)
    outs = []
    for h in heads:
        sl = slice(h * HEAD_DIM, (h + 1) * HEAD_DIM)
        st = st_ref[h]
        outs.append(_dot(scores[h], vb[:, sl]) + _dot_nt(query_tile(q_in, h), st.astype(BF16)))
        st_ref[h] = st * key_tile(state_decay, h) + _dot_tn(vb[:, sl], key_tile(k_out, h))
        side.step()
    return jnp.concatenate(outs, axis=1)


def _mlstm_chunk(q, k, v, log_i, log_f, c_ref, n_ref, m_ref, causal_bias, side):
    t = q.shape[0]
    heads = _head_slices()
    b = _prefix_scan_rows(log_f, jnp.add, 0.0)
    z = log_i - b
    zmax = _prefix_scan_rows(z, jnp.maximum, -jnp.inf)
    m_prev = m_ref[0:1, :]
    m_t = b + jnp.maximum(m_prev, zmax)
    col = b - m_t
    w_inter = jnp.exp(b + m_prev - m_t)
    inv_floor = jnp.exp(-m_t)
    z_rows = z.T
    b_last = b[t - 1:t, :]
    m_new = m_t[t - 1:t, :]
    w_state = jnp.exp(z + b_last - m_new)
    decay = jnp.exp(b_last + m_prev - m_new)
    qs = q * (HEAD_DIM ** -0.5)
    qb = qs.astype(BF16)
    kb = k.astype(BF16)
    vb = v.astype(BF16)
    n_prev = n_ref[0:1, :]
    outs, n_new = [], []
    half_t = t // 2
    top, bottom = slice(0, half_t), slice(half_t, t)
    zeros_q = jnp.zeros((half_t, half_t), BF16)
    for h, sl in enumerate(heads):
        hl = slice(h, h + 1)
        qk = _dot_nt(qb[:, sl], kb[:, sl])
        c, zr = col[:, hl], z_rows[hl, :]
        s_tl = qk[top, top] * jnp.exp(c[top] + zr[:, top] + causal_bias)
        s_bl = qk[bottom, top] * jnp.exp(c[bottom] + zr[:, top])
        s_br = qk[bottom, bottom] * jnp.exp(c[bottom] + zr[:, bottom] + causal_bias)
        s = jnp.concatenate([jnp.concatenate([s_tl.astype(BF16), zeros_q], axis=1),
                             jnp.concatenate([s_bl.astype(BF16), s_br.astype(BF16)], axis=1)], axis=0)
        s_sum = jnp.concatenate([jnp.sum(s_tl, axis=-1, keepdims=True),
                                 jnp.sum(s_bl, axis=-1, keepdims=True) + jnp.sum(s_br, axis=-1, keepdims=True)],
                                axis=0)
        c_h = c_ref[h]
        wi = w_inter[:, hl]
        num = _dot(s, vb[:, sl]) + wi * _dot(qb[:, sl], c_h.astype(BF16))
        den = s_sum + wi * jnp.sum(qs[:, sl] * n_prev[:, sl], axis=-1, keepdims=True)
        outs.append(num / jnp.maximum(jnp.abs(den), inv_floor[:, hl]))
        kw = k[:, sl] * w_state[:, hl]
        d_h = decay[:, hl]
        c_ref[h] = d_h * c_h + _dot_tn(kw.astype(BF16), vb[:, sl])
        n_new.append(d_h * n_prev[:, sl] + jnp.sum(kw, axis=0, keepdims=True))
        side.step(2)
    n_ref[0:1, :] = jnp.concatenate(n_new, axis=1)
    m_ref[0:1, :] = m_new
    return jnp.concatenate(outs, axis=1)


def _per_head(x, fn):
    return jnp.concatenate([fn(x[:, sl]) for sl in _head_slices()], axis=1)


def _rms_normalize(x):
    return x * lax.rsqrt(jnp.mean(x * x, axis=-1, keepdims=True) + NORM_EPS)


def _layer_normalize(x):
    xc = x - jnp.mean(x, axis=-1, keepdims=True)
    return xc * lax.rsqrt(jnp.mean(xc * xc, axis=-1, keepdims=True) + NORM_EPS)


def _layer_kernel(layer, final_norm,
                  x_ref, ng_ref, fg_ref, w_ref, wb_ref, wo_ref, lruw_ref, mqk_ref, mv_ref, wg_ref, bg_ref,
                  w2_ref, vec_ref, lbl_ref, lvl_ref, cb_ref,
                  o_ref,
                  conv_a, conv_b, lru_h, m_c, m_n, m_m, h_s, g_s):
    @pl.when(pl.program_id(1) == 0)
    def _reset_state():
        conv_a[...] = jnp.zeros(conv_a.shape, F32)
        conv_b[...] = jnp.zeros(conv_b.shape, F32)
        lru_h[...] = jnp.zeros(lru_h.shape, F32)
        m_c[...] = jnp.zeros(m_c.shape, F32)
        m_n[...] = jnp.zeros(m_n.shape, F32)
        m_m[...] = jnp.zeros(m_m.shape, F32)
        h_s[...] = jnp.zeros(h_s.shape, F32)
        g_s[...] = jnp.zeros(g_s.shape, F32)

    def tile(sub, carry):
        rows = pl.ds(pl.multiple_of(sub * TILE, TILE), TILE)
        _mix_tile(layer, final_norm, rows,
                  x_ref, ng_ref, fg_ref, w_ref, wb_ref, wo_ref, lruw_ref, mqk_ref, mv_ref, wg_ref, bg_ref,
                  w2_ref, vec_ref, lbl_ref, lvl_ref, cb_ref, o_ref, conv_a, conv_b, lru_h, m_c, m_n, m_m, h_s, g_s)
        return carry

    lax.fori_loop(0, TILES_PER_STEP, tile, 0)


def _mix_tile(layer, final_norm, rows,
              x_ref, ng_ref, fg_ref, w_ref, wb_ref, wo_ref, lruw_ref, mqk_ref, mv_ref, wg_ref, bg_ref,
              w2_ref, vec_ref, lbl_ref, lvl_ref, cb_ref, o_ref, conv_a, conv_b, lru_h, m_c, m_n, m_m, h_s, g_s):
    half = MXU_DIM

    def vec(row):
        return vec_ref[row:row + 1, :]

    lvl = lvl_ref[...]
    x = x_ref[rows, :]
    hb = (_rms_normalize(x) * ng_ref[...]).astype(BF16)
    side = _SideWork()

    def cat(pieces):
        return jnp.concatenate(pieces, axis=1)

    def queue_in_proj(lo, hi):
        pieces = []
        for c in range(lo, hi, PROJ_CHUNK):
            side.add(lambda c=c: pieces.append(_dot(hb, w_ref[:, c:min(c + PROJ_CHUNK, hi)])))
        return pieces

    def take_in_proj(pieces, lo, hi):
        side.run_until(pieces, -(-(hi - lo) // PROJ_CHUNK))
        return cat(pieces)

    def queue_merge_gate(branch):
        pieces = []
        base = COL_M + branch * D_MODEL
        for c in range(base, base + D_MODEL, PROJ_CHUNK):
            side.add(lambda c=c: pieces.append(_sigmoid(_dot(hb, w_ref[:, c:c + PROJ_CHUNK]))))
        return pieces

    def queue_branch_proj(branch, y):
        out = []
        yb = y.astype(BF16)
        side.add(lambda: out.append(_dot(yb, wb_ref[branch])))
        return out

    u = _dot(hb, w_ref[:, COL_A:COL_B])
    u_b, u_c, u_d = queue_in_proj(COL_B, COL_C), queue_in_proj(COL_C, COL_D), queue_in_proj(COL_D, COL_M)
    gate = [queue_merge_gate(n) for n in range(N_BRANCH)]

    lru_x, lru_z = u[:, :W_BRANCH], u[:, W_BRANCH:]
    xa = _causal_conv(lru_x, conv_a, [vec(ROW_LRU_CONV_W + j) for j in range(CONV_WIDTH)], vec(ROW_LRU_CONV_B))
    xab = xa.astype(BF16)
    g0 = _dot(xab[:, :half], lruw_ref[0])
    g1 = _dot(xab[:, half:], lruw_ref[1])
    side.step(2)
    r = _sigmoid(jnp.concatenate([g0[:, :half], g1[:, :half]], axis=1) + vec(ROW_LRU_BA))
    i = _sigmoid(jnp.concatenate([g0[:, half:], g1[:, half:]], axis=1) + vec(ROW_LRU_BX))
    log_a = (-LRU_C) * r * _softplus(-vec(ROW_LRU_LAMBDA))
    a = jnp.exp(log_a)
    side.step(2)
    h_lru = _linear_recurrence_rows(a, jnp.sqrt(1.0 - a * a) * (i * xa),
                                    lru_h[SUBLANES - 1:SUBLANES, :], side)
    lru_h[...] = h_lru[TILE - SUBLANES:, :]
    y_a = h_lru * _silu(lru_z)

    u = take_in_proj(u_b, COL_B, COL_C)
    proj_0 = queue_branch_proj(0, y_a)
    m_x, m_o, m_z = u[:, :W_BRANCH], u[:, W_BRANCH:2 * W_BRANCH], u[:, 2 * W_BRANCH:]
    xm = _silu(_causal_conv(m_x, conv_b, [vec(ROW_M_CONV_W + j) for j in range(CONV_WIDTH)], vec(ROW_M_CONV_B)))
    xmb = xm.astype(BF16)
    mxb = m_x.astype(BF16)
    qk0 = _dot(xmb[:, :half], mqk_ref[0])
    qk1 = _dot(xmb[:, half:], mqk_ref[1])
    q = jnp.concatenate([qk0[:, :half], qk1[:, :half]], axis=1)
    k = jnp.concatenate([qk0[:, half:], qk1[:, half:]], axis=1)
    v = jnp.concatenate([_dot(mxb[:, :half], mv_ref[0]), _dot(mxb[:, half:], mv_ref[1])], axis=1)
    side.step()
    gates = _dot(jnp.concatenate([q, k, v], axis=1).astype(BF16), wg_ref[...]) + bg_ref[...]
    hm = _mlstm_chunk(q, k, v, gates[:, :LANES], _log_sigmoid(gates[:, LANES:]), m_c, m_n, m_m, cb_ref[...], side)
    hm = _sigmoid(m_o) * hm
    y_b = (_per_head(hm, _layer_normalize) * vec(ROW_M_NORM_W) + vec(ROW_M_SKIP) * xm) * _silu(m_z)

    u = take_in_proj(u_c, COL_C, COL_D)
    proj_1 = queue_branch_proj(1, y_b)
    h_q, h_f = u[:, :W_BRANCH], u[:, W_BRANCH:2 * W_BRANCH]
    h_i, h_z = u[:, 2 * W_BRANCH:3 * W_BRANCH], u[:, 3 * W_BRANCH:]
    e_f = jnp.exp(-jnp.abs(h_f))
    r_f = 1.0 / (1.0 + e_f)
    h_k = jnp.where(h_f >= 0, e_f * r_f, r_f)
    log_fh = jnp.minimum(h_f, 0.0) - jnp.log(1.0 + e_f)
    if layer > 0:
        logits = lbl_ref[...]
        ex = jnp.exp(logits - jnp.max(logits, axis=0, keepdims=True))
        sm = ex / jnp.sum(ex, axis=0, keepdims=True)
        lb = sm[1:2, :]
        for j in range(2, layer + 1):
            lb = lb + sm[j:j + 1, :]
        h_k = (1.0 - lb) * h_k
        lo = jnp.log(lb)
        hi = jnp.log1p(-lb) + log_fh
        log_fh = jnp.maximum(lo, hi) + jnp.log(1.0 + jnp.exp(-jnp.abs(lo - hi)))
    side.step()
    oh = _gla_chunk(_silu(h_q), h_k, h_i, log_fh, h_s, lvl, HEAD_DIM, side)
    y_c = _per_head(oh, _rms_normalize) * vec(ROW_H_NORM_W) * _silu(h_z)

    u = take_in_proj(u_d, COL_D, COL_M)
    proj_2 = queue_branch_proj(2, y_c)
    g_q, g_k, g_v = u[:, :GLA_DK], u[:, GLA_DK:2 * GLA_DK], u[:, 2 * GLA_DK:2 * GLA_DK + W_BRANCH]
    g_lr = u[:, 2 * GLA_DK + W_BRANCH:2 * GLA_DK + W_BRANCH + LANES]
    g_z = u[:, 2 * GLA_DK + W_BRANCH + LANES:]
    log_gk = (_log_sigmoid(_dot(g_lr.astype(BF16), w2_ref[...]) + vec(ROW_G_B_LR2)[:, :GLA_DK])
              * (1.0 / GLA_TAU))
    og = _gla_chunk(g_q, g_k, g_v, log_gk, g_s, lvl, GLA_HEAD_DK, side)
    y_d = _per_head(og, _rms_normalize) * vec(ROW_G_NORM_W) * _silu(g_z)

    side.drain()
    merged = cat(gate[0]) * proj_0[0] + cat(gate[1]) * proj_1[0] + cat(gate[2]) * proj_2[0]
    merged = merged + cat(gate[3]) * _dot(y_d.astype(BF16), wb_ref[3])
    y = x + _dot(merged.astype(BF16), wo_ref[...])
    if final_norm:
        y = _rms_normalize(y) * fg_ref[...]
    o_ref[rows, :] = y


def _block_diag_tiles(w):
    n, b, _ = w.shape
    rows = w.astype(BF16).reshape(n * b // MXU_DIM, MXU_DIM, b)
    col = np.arange(MXU_DIM)
    repeat = (col[None, :] % b == np.arange(b)[:, None]).astype(np.float32)
    repeated = jnp.einsum('trj,jc->trc', rows, jnp.asarray(repeat, BF16), preferred_element_type=F32)
    own_block = col[:, None] // b == col[None, :] // b
    return jnp.where(own_block, repeated, 0.0).astype(BF16)


def _level_table(t):
    idx = np.arange(t)
    xor = idx[:, None] ^ idx[None, :]
    lv = np.floor(np.log2(np.maximum(xor, 1))).astype(np.int32) + 1
    lv = np.where(idx[:, None] == idx[None, :], 0, lv)
    return np.where(idx[:, None] >= idx[None, :], lv, -1).astype(np.float32)


def _causal_bias(t):
    idx = np.arange(t)
    return np.where(idx[:, None] >= idx[None, :], 0.0, -np.inf).astype(np.float32)


def _pack_in_proj_kernel(depth, wt_ref, *refs):
    outs, carry = refs[:depth], refs[depth]
    j = pl.program_id(0)
    lr_block = (COL_LR + LANES) // PACK_COLS - 1
    real_rows = COL_LR + GLA_RANK - lr_block * PACK_COLS
    row = lax.broadcasted_iota(jnp.int32, (PACK_COLS, D_MODEL), 0)

    @pl.when(j == 0)
    def _init_carry():
        carry[...] = jnp.zeros(carry.shape, F32)

    for l in range(depth):
        x = wt_ref[l]
        shifted = jnp.concatenate([carry[l], x[:PACK_COLS - PACK_PAD]], axis=0)
        padded = jnp.where(row < real_rows, x, 0.0)
        y = jnp.where(j > lr_block, shifted, jnp.where(j == lr_block, padded, x))
        outs[l][...] = y.T.astype(BF16)
        carry[l] = x[PACK_COLS - PACK_PAD:]


def _pack_in_proj(w_in):
    depth, _, d_in = w_in.shape
    assert d_in == sum(SPLIT_SIZES) and COL_END == d_in + PACK_PAD
    assert COL_END % PACK_COLS == 0 and (COL_LR + LANES) % PACK_COLS == 0
    last_block = (d_in - 1) // PACK_COLS
    return pl.pallas_call(
        functools.partial(_pack_in_proj_kernel, depth),
        grid=(COL_END // PACK_COLS,),
        in_specs=[pl.BlockSpec((depth, PACK_COLS, D_MODEL), lambda j: (0, jnp.minimum(j, last_block), 0))],
        out_specs=[pl.BlockSpec((D_MODEL, PACK_COLS), lambda j: (0, j))] * depth,
        out_shape=[jax.ShapeDtypeStruct((D_MODEL, COL_END), BF16)] * depth,
        scratch_shapes=[pltpu.VMEM((depth, PACK_PAD, D_MODEL), F32)],
        compiler_params=pltpu.CompilerParams(dimension_semantics=("arbitrary",),
                                             vmem_limit_bytes=VMEM_LIMIT_BYTES),
        name="pack_in_proj",
    )(jnp.swapaxes(w_in, 1, 2))


def _pack_out_proj_kernel(depth, w_br_ref, w_o_ref, *out_refs):
    for l in range(depth):
        out_refs[l][...] = w_br_ref[l].astype(BF16)
        out_refs[depth + l][...] = w_o_ref[l].astype(BF16)


def _pack_out_proj(w_branch, w_out):
    depth = w_branch.shape[0]
    n_steps = D_MODEL // PACK_ROWS
    br_rows = N_BRANCH * W_BRANCH // n_steps
    outs = pl.pallas_call(
        functools.partial(_pack_out_proj_kernel, depth),
        grid=(n_steps,),
        in_specs=[pl.BlockSpec((depth, br_rows, D_MODEL), lambda r: (0, r, 0)),
                  pl.BlockSpec((depth, PACK_ROWS, D_MODEL), lambda r: (0, r, 0))],
        out_specs=([pl.BlockSpec((br_rows, D_MODEL), lambda r: (r, 0))] * depth
                   + [pl.BlockSpec((PACK_ROWS, D_MODEL), lambda r: (r, 0))] * depth),
        out_shape=([jax.ShapeDtypeStruct((N_BRANCH * W_BRANCH, D_MODEL), BF16)] * depth
                   + [jax.ShapeDtypeStruct((D_MODEL, D_MODEL), BF16)] * depth),
        compiler_params=pltpu.CompilerParams(dimension_semantics=("arbitrary",)),
        name="pack_out_proj",
    )(w_branch.reshape(depth, N_BRANCH * W_BRANCH, D_MODEL), w_out)
    return [w.reshape(N_BRANCH, W_BRANCH, D_MODEL) for w in outs[:depth]], outs[depth:]


def _pack_layer(l, lru_conv_w, lru_conv_b, lru_wa, lru_ba, lru_wx, lru_bx, lru_lambda,
                m_conv_w, m_conv_b, m_wq, m_wk, m_wv, m_wi, m_bi, m_wf, m_bf, m_norm_w, m_skip,
                h_norm_w, g_w_lr2, g_b_lr2, g_norm_w):
    lru_w = jnp.concatenate([_block_diag_tiles(lru_wa[l]), _block_diag_tiles(lru_wx[l])], axis=2)
    m_qk = jnp.concatenate([_block_diag_tiles(m_wq[l]), _block_diag_tiles(m_wk[l])], axis=2)
    m_v = _block_diag_tiles(m_wv[l])
    w_gate = jnp.zeros((3 * W_BRANCH, 2 * LANES), F32)
    w_gate = w_gate.at[:, :N_HEADS].set(m_wi[l]).at[:, LANES:LANES + N_HEADS].set(m_wf[l])
    b_gate = jnp.zeros((1, 2 * LANES), F32)
    b_gate = b_gate.at[0, :N_HEADS].set(m_bi[l]).at[0, LANES:LANES + N_HEADS].set(m_bf[l])
    w_lr2 = jnp.pad(g_w_lr2[l], ((0, LANES - GLA_RANK), (0, 0)))
    rows = [lru_conv_w[l][j] for j in range(CONV_WIDTH)]
    rows += [lru_conv_b[l], lru_ba[l], lru_bx[l], lru_lambda[l]]
    rows += [m_conv_w[l][j] for j in range(CONV_WIDTH)]
    rows += [m_conv_b[l], m_norm_w[l], m_skip[l], jnp.tile(h_norm_w[l], N_HEADS), jnp.tile(g_norm_w[l], N_HEADS),
             jnp.pad(g_b_lr2[l], (0, W_BRANCH - GLA_DK))]
    vecs = jnp.stack(rows).astype(F32)
    vecs = jnp.pad(vecs, ((0, N_VEC_ROWS - vecs.shape[0]), (0, 0)))
    return lru_w, m_qk, m_v, w_gate.astype(BF16), b_gate, w_lr2.astype(BF16), vecs


def _layer_call(layer, final_norm, batch, seq):
    block_rows = TILE * TILES_PER_STEP
    n_tiles = seq // block_rows
    tile_spec = pl.BlockSpec((block_rows, D_MODEL), lambda b, j: (b * n_tiles + j, 0))
    resident = pl.BlockSpec(memory_space=pltpu.VMEM)
    state = functools.partial(pltpu.VMEM, dtype=F32)
    return pl.pallas_call(
        functools.partial(_layer_kernel, layer, final_norm),
        grid=(batch, n_tiles),
        in_specs=[tile_spec] + [resident] * 15,
        out_specs=tile_spec,
        out_shape=jax.ShapeDtypeStruct((batch * seq, D_MODEL), F32),
        scratch_shapes=[
            state((SUBLANES, W_BRANCH)),
            state((SUBLANES, W_BRANCH)),
            state((SUBLANES, W_BRANCH)),
            state((N_HEADS, HEAD_DIM, HEAD_DIM)),
            state((SUBLANES, W_BRANCH)),
            state((SUBLANES, LANES)),
            state((N_HEADS, HEAD_DIM, HEAD_DIM)),
            state((N_HEADS, HEAD_DIM, LANES)),
        ],
        compiler_params=pltpu.CompilerParams(
            dimension_semantics=("arbitrary", "arbitrary"),
            vmem_limit_bytes=VMEM_LIMIT_BYTES,
        ),
        name=f"hybrid_layer_{layer}",
    )


def kernel(x, norm_g, w_in, lru_conv_w, lru_conv_b, lru_wa, lru_ba, lru_wx, lru_bx, lru_lambda, m_conv_w, m_conv_b, m_wq, m_wk, m_wv, m_wi, m_bi, m_wf, m_bf, m_norm_w, m_skip, h_lb_logits, h_norm_w, g_w_lr2, g_b_lr2, g_norm_w, w_branch, w_out, final_g):
    batch, seq, d = x.shape
    depth = w_in.shape[0]
    assert d == D_MODEL and seq % (TILE * TILES_PER_STEP) == 0
    lvl = jnp.asarray(_level_table(TILE // 2), BF16)
    causal_bias = jnp.asarray(_causal_bias(TILE // 2))
    lb_logits = h_lb_logits.astype(F32)
    w_all = _pack_in_proj(w_in)
    w_br, w_o = _pack_out_proj(w_branch, w_out)
    y = x.reshape(batch * seq, d).astype(F32)
    for l in range(depth):
        packed = _pack_layer(l, lru_conv_w, lru_conv_b, lru_wa, lru_ba, lru_wx, lru_bx, lru_lambda,
                             m_conv_w, m_conv_b, m_wq, m_wk, m_wv, m_wi, m_bi, m_wf, m_bf, m_norm_w, m_skip,
                             h_norm_w, g_w_lr2, g_b_lr2, g_norm_w)
        (lru_w, m_qk, m_v, w_gate, b_gate, w_lr2, vecs) = packed
        y = _layer_call(l, l == depth - 1, batch, seq)(
            y, norm_g[l].reshape(1, d), final_g.reshape(1, d), w_all[l], w_br[l], w_o[l], lru_w, m_qk, m_v, w_gate,
            b_gate, w_lr2, vecs, lb_logits, lvl, causal_bias)
    return y.reshape(batch, seq, d)
```

```python
import collections
import functools

import numpy as np
import jax
import jax.numpy as jnp
from jax import lax
from jax.experimental import pallas as pl
from jax.experimental.pallas import tpu as pltpu

D_MODEL = 1024
W_BRANCH = 512
N_BRANCH = 4
CONV_WIDTH = 4
LRU_BLOCKS = 8
LRU_C = 8.0
N_HEADS = 4
HEAD_DIM = W_BRANCH // N_HEADS
QKV_BLOCK = 4
GLA_DK = 256
GLA_HEAD_DK = GLA_DK // N_HEADS
GLA_RANK = 16
GLA_TAU = 16.0
NORM_EPS = 1e-6
SPLIT_SIZES = (512, 512, 512, 512, 512, 512, 512, 512, 512, 256, 256, 512, 16, 512, 4096)

LANES = 128
SUBLANES = 8
MXU_DIM = 256
VMEM_LIMIT_BYTES = 56 * 1024 * 1024

TILE = 256
TILES_PER_STEP = 2
PROJ_CHUNK = MXU_DIM

COL_A = 0
COL_B = COL_A + 2 * W_BRANCH
COL_C = COL_B + 3 * W_BRANCH
COL_D = COL_C + 4 * W_BRANCH
COL_LR = COL_D + 2 * GLA_DK + W_BRANCH
COL_M = COL_LR + LANES + W_BRANCH
COL_END = COL_M + N_BRANCH * D_MODEL
PACK_PAD = LANES - GLA_RANK
PACK_COLS = 3 * LANES
PACK_ROWS = 128

ROW_LRU_CONV_W = 0
ROW_LRU_CONV_B = 4
ROW_LRU_BA = 5
ROW_LRU_BX = 6
ROW_LRU_LAMBDA = 7
ROW_M_CONV_W = 8
ROW_M_CONV_B = 12
ROW_M_NORM_W = 13
ROW_M_SKIP = 14
ROW_H_NORM_W = 15
ROW_G_NORM_W = 16
ROW_G_B_LR2 = 17
N_VEC_ROWS = 24

BF16 = jnp.bfloat16
F32 = jnp.float32


class _SideWork:
    def __init__(self):
        self._items = collections.deque()

    def add(self, thunk):
        self._items.append(thunk)

    def step(self, n=1):
        for _ in range(n):
            if self._items:
                self._items.popleft()()

    def run_until(self, pieces, count):
        while len(pieces) < count:
            self._items.popleft()()

    def drain(self):
        while self._items:
            self._items.popleft()()


def _dot(a, b):
    return jnp.dot(a, b, preferred_element_type=F32)


def _dot_nt(a, b):
    return lax.dot_general(a, b, (((1,), (1,)), ((), ())), preferred_element_type=F32)


def _dot_tn(a, b):
    return lax.dot_general(a, b, (((0,), (0,)), ((), ())), preferred_element_type=F32)


def _sigmoid(x):
    return jax.nn.sigmoid(x)


def _silu(x):
    return x * _sigmoid(x)


def _log_sigmoid(x):
    return jnp.minimum(x, 0.0) - jnp.log(1.0 + jnp.exp(-jnp.abs(x)))


def _softplus(x):
    return jnp.maximum(x, 0.0) + jnp.log(1.0 + jnp.exp(-jnp.abs(x)))


def _to_tiles(x):
    t, c = x.shape
    return x.reshape(t // SUBLANES, SUBLANES, c)


def _from_tiles(x3):
    nb, s, c = x3.shape
    return x3.reshape(nb * s, c)


def _sub_iota(shape):
    return lax.broadcasted_iota(jnp.int32, shape, 1)


def _block_ref(x, b):
    t, c = x.shape
    if b == 1:
        x3 = _to_tiles(x)
        return _from_tiles(jnp.where((_sub_iota(x3.shape) & 1) == 1, pltpu.roll(x3, 1, 1), x3))
    if b == 2:
        x3 = _to_tiles(x)
        m = _sub_iota(x3.shape) & 3
        return _from_tiles(jnp.where(m == 0, pltpu.roll(x3, SUBLANES - 1, 1),
                                     jnp.where(m == 1, x3,
                                               jnp.where(m == 2, pltpu.roll(x3, 1, 1), pltpu.roll(x3, 2, 1)))))
    nb = t // (2 * b)
    x3 = x.reshape(nb, 2 * b, c)
    return jnp.broadcast_to(x3[:, b - 1:b, :], (nb, 2 * b, c)).reshape(t, c)


def _prefix_scan_rows(x, combine, identity):
    x3 = _to_tiles(x)
    nb, _, c = x3.shape
    j = _sub_iota(x3.shape)
    for k in (1, 2, 4):
        x3 = combine(jnp.where(j >= k, pltpu.roll(x3, k, 1), identity), x3)
    totals = x3[:, SUBLANES - 1:, :]
    carry = jnp.full((1, 1, c), identity, x.dtype)
    carries = []
    for i in range(nb):
        carries.append(carry)
        carry = combine(carry, totals[i:i + 1])
    return _from_tiles(combine(jnp.concatenate(carries, axis=0), x3))


def _linear_recurrence_rows(a, u, h_prev, side):
    a3, u3 = _to_tiles(a), _to_tiles(u)
    nb, _, c = a3.shape
    j = _sub_iota(a3.shape)
    for k in (1, 2, 4):
        a_back = jnp.where(j >= k, pltpu.roll(a3, k, 1), 1.0)
        u_back = jnp.where(j >= k, pltpu.roll(u3, k, 1), 0.0)
        u3 = a3 * u_back + u3
        a3 = a3 * a_back
    side.step(2)
    a_tot, u_tot = a3[:, SUBLANES - 1:, :], u3[:, SUBLANES - 1:, :]
    h_in = h_prev.reshape(1, 1, c)
    carries = []
    for i in range(nb):
        carries.append(h_in)
        h_in = u_tot[i:i + 1] + a_tot[i:i + 1] * h_in
    return _from_tiles(u3 + a3 * jnp.concatenate(carries, axis=0))


def _causal_conv(x, tail_ref, w, bias):
    t, c = x.shape
    x3 = _to_tiles(x)
    prev3 = jnp.concatenate([tail_ref[...].reshape(1, SUBLANES, c), x3[:-1]], axis=0)
    j = _sub_iota(x3.shape)
    y = bias + w[CONV_WIDTH - 1] * x
    for back in range(1, CONV_WIDTH):
        shifted = pltpu.roll(jnp.where(j >= SUBLANES - back, prev3, x3), back, 1)
        y = y + w[CONV_WIDTH - 1 - back] * _from_tiles(shifted)
    tail_ref[...] = x[t - SUBLANES:, :]
    return y


def _head_slices():
    return [slice(h * HEAD_DIM, (h + 1) * HEAD_DIM) for h in range(N_HEADS)]


def _gla_chunk(q, k, v, g, st_ref, lvl, head_dk, side):
    t = q.shape[0]
    per_tile = LANES // head_dk
    lane = lax.broadcasted_iota(jnp.int32, (1, LANES), 1)

    def key_tile(x, h):
        tile = h // per_tile
        return x[:, tile * LANES:(tile + 1) * LANES]

    def query_tile(x, h):
        if per_tile == 1:
            return key_tile(x, h)
        own = ((lane // head_dk) == (h % per_tile)).astype(BF16)
        return key_tile(x, h) * own

    q = q * (head_dk ** -0.5)
    vb = v.astype(BF16)
    qb = q.astype(BF16)
    kb = k.astype(BF16)
    heads = range(N_HEADS)
    zero = jnp.zeros((), BF16)

    def diagonal(h):
        prod = key_tile(q, h) * key_tile(k, h)
        if per_tile > 1:
            prod = jnp.where((lane // head_dk) == (h % per_tile), prod, 0.0)
        return jnp.sum(prod, axis=-1, keepdims=True)

    half_t = t // 2
    top, bottom = slice(0, half_t), slice(half_t, t)

    def diag_quadrants(full):
        return full[top, top], full[bottom, bottom]

    def diag_quadrant_scores(h):
        d = diagonal(h)
        return [jnp.where(lvl == 0, jnp.broadcast_to(d[rows], (half_t, half_t)).astype(BF16), zero)
                for rows in (top, bottom)]

    scores = [diag_quadrant_scores(h) for h in heads]
    lower_left = [None] * N_HEADS
    g_cum = _prefix_scan_rows(g, jnp.add, 0.0)
    b, level = 1, 1
    while b < t:
        eb = jnp.exp(-jnp.abs(g_cum - _block_ref(g_cum, b))).astype(BF16)
        qe = qb * eb
        ke = kb * eb
        for h in heads:
            if b == half_t:
                lower_left[h] = _dot_nt(query_tile(qe, h)[bottom], key_tile(ke, h)[top]).astype(BF16)
            else:
                full = _dot_nt(query_tile(qe, h), key_tile(ke, h))
                scores[h] = [jnp.where(lvl == level, a.astype(BF16), s)
                             for a, s in zip(diag_quadrants(full), scores[h])]
        side.step()
        b *= 2
        level += 1
    zeros_q = jnp.zeros((half_t, half_t), BF16)
    scores = [jnp.concatenate([jnp.concatenate([scores[h][0], zeros_q], axis=1),
                               jnp.concatenate([lower_left[h], scores[h][1]], axis=1)], axis=0) for h in heads]
    g_last = g_cum[t - 1:t, :]
    q_in = (q * jnp.exp(g_cum)).astype(BF16)
    k_out = (k * jnp.exp(g_last - g_cum)).astype(BF16)
    state_decay = jnp.exp(g_last)
    outs = []
    for h in heads:
        sl = slice(h * HEAD_DIM, (h + 1) * HEAD_DIM)
        st = st_ref[h]
        outs.append(_dot(scores[h], vb[:, sl]) + _dot_nt(query_tile(q_in, h), st.astype(BF16)))
        st_ref[h] = st * key_tile(state_decay, h) + _dot_tn(vb[:, sl], key_tile(k_out, h))
        side.step()
    return jnp.concatenate(outs, axis=1)


def _mlstm_chunk(q, k, v, log_i, log_f, c_ref, n_ref, m_ref, causal_bias, side):
    t = q.shape[0]
    heads = _head_slices()
    b = _prefix_scan_rows(log_f, jnp.add, 0.0)
    z = log_i - b
    zmax = _prefix_scan_rows(z, jnp.maximum, -jnp.inf)
    m_prev = m_ref[0:1, :]
    m_t = b + jnp.maximum(m_prev, zmax)
    col = b - m_t
    w_inter = jnp.exp(b + m_prev - m_t)
    inv_floor = jnp.exp(-m_t)
    z_rows = z.T
    b_last = b[t - 1:t, :]
    m_new = m_t[t - 1:t, :]
    w_state = jnp.exp(z + b_last - m_new)
    decay = jnp.exp(b_last + m_prev - m_new)
    qs = q * (HEAD_DIM ** -0.5)
    qb = qs.astype(BF16)
    kb = k.astype(BF16)
    vb = v.astype(BF16)
    n_prev = n_ref[0:1, :]
    outs, n_new = [], []
    half_t = t // 2
    top, bottom = slice(0, half_t), slice(half_t, t)
    zeros_q = jnp.zeros((half_t, half_t), BF16)
    for h, sl in enumerate(heads):
        hl = slice(h, h + 1)
        qk = _dot_nt(qb[:, sl], kb[:, sl])
        c, zr = col[:, hl], z_rows[hl, :]
        s_tl = qk[top, top] * jnp.exp(c[top] + zr[:, top] + causal_bias)
        s_bl = qk[bottom, top] * jnp.exp(c[bottom] + zr[:, top])
        s_br = qk[bottom, bottom] * jnp.exp(c[bottom] + zr[:, bottom] + causal_bias)
        s = jnp.concatenate([jnp.concatenate([s_tl.astype(BF16), zeros_q], axis=1),
                             jnp.concatenate([s_bl.astype(BF16), s_br.astype(BF16)], axis=1)], axis=0)
        s_sum = jnp.concatenate([jnp.sum(s_tl, axis=-1, keepdims=True),
                                 jnp.sum(s_bl, axis=-1, keepdims=True) + jnp.sum(s_br, axis=-1, keepdims=True)],
                                axis=0)
        c_h = c_ref[h]
        wi = w_inter[:, hl]
        num = _dot(s, vb[:, sl]) + wi * _dot(qb[:, sl], c_h.astype(BF16))
        den = s_sum + wi * jnp.sum(qs[:, sl] * n_prev[:, sl], axis=-1, keepdims=True)
        outs.append(num / jnp.maximum(jnp.abs(den), inv_floor[:, hl]))
        kw = k[:, sl] * w_state[:, hl]
        d_h = decay[:, hl]
        c_ref[h] = d_h * c_h + _dot_tn(kw.astype(BF16), vb[:, sl])
        n_new.append(d_h * n_prev[:, sl] + jnp.sum(kw, axis=0, keepdims=True))
        side.step(2)
    n_ref[0:1, :] = jnp.concatenate(n_new, axis=1)
    m_ref[0:1, :] = m_new
    return jnp.concatenate(outs, axis=1)


def _per_head(x, fn):
    return jnp.concatenate([fn(x[:, sl]) for sl in _head_slices()], axis=1)


def _rms_normalize(x):
    return x * lax.rsqrt(jnp.mean(x * x, axis=-1, keepdims=True) + NORM_EPS)


def _layer_normalize(x):
    xc = x - jnp.mean(x, axis=-1, keepdims=True)
    return xc * lax.rsqrt(jnp.mean(xc * xc, axis=-1, keepdims=True) + NORM_EPS)


def _layer_kernel(layer, final_norm,
                  x_ref, ng_ref, fg_ref, w_ref, wb_ref, wo_ref, lruw_ref, mqk_ref, mv_ref, wg_ref, bg_ref,
                  w2_ref, vec_ref, lbl_ref, lvl_ref, cb_ref,
                  o_ref,
                  conv_a, conv_b, lru_h, m_c, m_n, m_m, h_s, g_s):
    @pl.when(pl.program_id(1) == 0)
    def _reset_state():
        conv_a[...] = jnp.zeros(conv_a.shape, F32)
        conv_b[...] = jnp.zeros(conv_b.shape, F32)
        lru_h[...] = jnp.zeros(lru_h.shape, F32)
        m_c[...] = jnp.zeros(m_c.shape, F32)
        m_n[...] = jnp.zeros(m_n.shape, F32)
        m_m[...] = jnp.zeros(m_m.shape, F32)
        h_s[...] = jnp.zeros(h_s.shape, F32)
        g_s[...] = jnp.zeros(g_s.shape, F32)

    def tile(sub, carry):
        rows = pl.ds(pl.multiple_of(sub * TILE, TILE), TILE)
        _mix_tile(layer, final_norm, rows,
                  x_ref, ng_ref, fg_ref, w_ref, wb_ref, wo_ref, lruw_ref, mqk_ref, mv_ref, wg_ref, bg_ref,
                  w2_ref, vec_ref, lbl_ref, lvl_ref, cb_ref, o_ref, conv_a, conv_b, lru_h, m_c, m_n, m_m, h_s, g_s)
        return carry

    lax.fori_loop(0, TILES_PER_STEP, tile, 0)


def _mix_tile(layer, final_norm, rows,
              x_ref, ng_ref, fg_ref, w_ref, wb_ref, wo_ref, lruw_ref, mqk_ref, mv_ref, wg_ref, bg_ref,
              w2_ref, vec_ref, lbl_ref, lvl_ref, cb_ref, o_ref, conv_a, conv_b, lru_h, m_c, m_n, m_m, h_s, g_s):
    half = MXU_DIM

    def vec(row):
        return vec_ref[row:row + 1, :]

    lvl = lvl_ref[...]
    x = x_ref[rows, :]
    hb = (_rms_normalize(x) * ng_ref[...]).astype(BF16)
    side = _SideWork()

    def cat(pieces):
        return jnp.concatenate(pieces, axis=1)

    def queue_in_proj(lo, hi):
        pieces = []
        for c in range(lo, hi, PROJ_CHUNK):
            side.add(lambda c=c: pieces.append(_dot(hb, w_ref[:, c:min(c + PROJ_CHUNK, hi)])))
        return pieces

    def take_in_proj(pieces, lo, hi):
        side.run_until(pieces, -(-(hi - lo) // PROJ_CHUNK))
        return cat(pieces)

    def queue_merge_gate(branch):
        pieces = []
        base = COL_M + branch * D_MODEL
        for c in range(base, base + D_MODEL, PROJ_CHUNK):
            side.add(lambda c=c: pieces.append(_sigmoid(_dot(hb, w_ref[:, c:c + PROJ_CHUNK]))))
        return pieces

    def queue_branch_proj(branch, y):
        out = []
        yb = y.astype(BF16)
        side.add(lambda: out.append(_dot(yb, wb_ref[branch])))
        return out

    u = _dot(hb, w_ref[:, COL_A:COL_B])
    u_b, u_c, u_d = queue_in_proj(COL_B, COL_C), queue_in_proj(COL_C, COL_D), queue_in_proj(COL_D, COL_M)
    gate = [queue_merge_gate(n) for n in range(N_BRANCH)]

    lru_x, lru_z = u[:, :W_BRANCH], u[:, W_BRANCH:]
    xa = _causal_conv(lru_x, conv_a, [vec(ROW_LRU_CONV_W + j) for j in range(CONV_WIDTH)], vec(ROW_LRU_CONV_B))
    xab = xa.astype(BF16)
    g0 = _dot(xab[:, :half], lruw_ref[0])
    g1 = _dot(xab[:, half:], lruw_ref[1])
    side.step(2)
    gate_r = [g0[:, :half], g1[:, :half]]
    gate_i = [g0[:, half:], g1[:, half:]]
    decay_rate = (-LRU_C) * _softplus(-vec(ROW_LRU_LAMBDA))
    strips, tails = [], []
    for c0 in range(0, W_BRANCH, LANES):
        cs = slice(c0, c0 + LANES)
        gs = slice(c0 % half, c0 % half + LANES)
        r = _sigmoid(gate_r[c0 // half][:, gs] + vec(ROW_LRU_BA)[:, cs])
        i = _sigmoid(gate_i[c0 // half][:, gs] + vec(ROW_LRU_BX)[:, cs])
        a = jnp.exp(decay_rate[:, cs] * r)
        h = _linear_recurrence_rows(a, jnp.sqrt(1.0 - a * a) * (i * xa[:, cs]),
                                    lru_h[SUBLANES - 1:SUBLANES, cs], side)
        tails.append(h[TILE - SUBLANES:, :])
        strips.append(h * _silu(lru_z[:, cs]))
    lru_h[...] = jnp.concatenate(tails, axis=1)
    y_a = jnp.concatenate(strips, axis=1)

    u = take_in_proj(u_b, COL_B, COL_C)
    proj_0 = queue_branch_proj(0, y_a)
    m_x, m_o, m_z = u[:, :W_BRANCH], u[:, W_BRANCH:2 * W_BRANCH], u[:, 2 * W_BRANCH:]
    xm = _silu(_causal_conv(m_x, conv_b, [vec(ROW_M_CONV_W + j) for j in range(CONV_WIDTH)], vec(ROW_M_CONV_B)))
    xmb = xm.astype(BF16)
    mxb = m_x.astype(BF16)
    qk0 = _dot(xmb[:, :half], mqk_ref[0])
    qk1 = _dot(xmb[:, half:], mqk_ref[1])
    q = jnp.concatenate([qk0[:, :half], qk1[:, :half]], axis=1)
    k = jnp.concatenate([qk0[:, half:], qk1[:, half:]], axis=1)
    v = jnp.concatenate([_dot(mxb[:, :half], mv_ref[0]), _dot(mxb[:, half:], mv_ref[1])], axis=1)
    side.step()
    gates = _dot(jnp.concatenate([q, k, v], axis=1).astype(BF16), wg_ref[...]) + bg_ref[...]
    hm = _mlstm_chunk(q, k, v, gates[:, :LANES], _log_sigmoid(gates[:, LANES:]), m_c, m_n, m_m, cb_ref[...], side)
    hm = _sigmoid(m_o) * hm
    y_b = (_per_head(hm, _layer_normalize) * vec(ROW_M_NORM_W) + vec(ROW_M_SKIP) * xm) * _silu(m_z)

    u = take_in_proj(u_c, COL_C, COL_D)
    proj_1 = queue_branch_proj(1, y_b)
    h_q, h_f = u[:, :W_BRANCH], u[:, W_BRANCH:2 * W_BRANCH]
    h_i, h_z = u[:, 2 * W_BRANCH:3 * W_BRANCH], u[:, 3 * W_BRANCH:]
    e_f = jnp.exp(-jnp.abs(h_f))
    r_f = 1.0 / (1.0 + e_f)
    h_k = jnp.where(h_f >= 0, e_f * r_f, r_f)
    log_fh = jnp.minimum(h_f, 0.0) - jnp.log(1.0 + e_f)
    if layer > 0:
        logits = lbl_ref[...]
        ex = jnp.exp(logits - jnp.max(logits, axis=0, keepdims=True))
        sm = ex / jnp.sum(ex, axis=0, keepdims=True)
        lb = sm[1:2, :]
        for j in range(2, layer + 1):
            lb = lb + sm[j:j + 1, :]
        h_k = (1.0 - lb) * h_k
        lo = jnp.log(lb)
        hi = jnp.log1p(-lb) + log_fh
        log_fh = jnp.maximum(lo, hi) + jnp.log(1.0 + jnp.exp(-jnp.abs(lo - hi)))
    side.step()
    oh = _gla_chunk(_silu(h_q), h_k, h_i, log_fh, h_s, lvl, HEAD_DIM, side)
    y_c = _per_head(oh, _rms_normalize) * vec(ROW_H_NORM_W) * _silu(h_z)

    u = take_in_proj(u_d, COL_D, COL_M)
    proj_2 = queue_branch_proj(2, y_c)
    g_q, g_k, g_v = u[:, :GLA_DK], u[:, GLA_DK:2 * GLA_DK], u[:, 2 * GLA_DK:2 * GLA_DK + W_BRANCH]
    g_lr = u[:, 2 * GLA_DK + W_BRANCH:2 * GLA_DK + W_BRANCH + LANES]
    g_z = u[:, 2 * GLA_DK + W_BRANCH + LANES:]
    log_gk = (_log_sigmoid(_dot(g_lr.astype(BF16), w2_ref[...]) + vec(ROW_G_B_LR2)[:, :GLA_DK])
              * (1.0 / GLA_TAU))
    og = _gla_chunk(g_q, g_k, g_v, log_gk, g_s, lvl, GLA_HEAD_DK, side)
    y_d = _per_head(og, _rms_normalize) * vec(ROW_G_NORM_W) * _silu(g_z)

    side.drain()
    merged = cat(gate[0]) * proj_0[0] + cat(gate[1]) * proj_1[0] + cat(gate[2]) * proj_2[0]
    merged = merged + cat(gate[3]) * _dot(y_d.astype(BF16), wb_ref[3])
    y = x + _dot(merged.astype(BF16), wo_ref[...])
    if final_norm:
        y = _rms_normalize(y) * fg_ref[...]
    o_ref[rows, :] = y


def _block_diag_tiles(w):
    n, b, _ = w.shape
    rows = w.astype(BF16).reshape(n * b // MXU_DIM, MXU_DIM, b)
    col = np.arange(MXU_DIM)
    repeat = (col[None, :] % b == np.arange(b)[:, None]).astype(np.float32)
    repeated = jnp.einsum('trj,jc->trc', rows, jnp.asarray(repeat, BF16), preferred_element_type=F32)
    own_block = col[:, None] // b == col[None, :] // b
    return jnp.where(own_block, repeated, 0.0).astype(BF16)


def _level_table(t):
    idx = np.arange(t)
    xor = idx[:, None] ^ idx[None, :]
    lv = np.floor(np.log2(np.maximum(xor, 1))).astype(np.int32) + 1
    lv = np.where(idx[:, None] == idx[None, :], 0, lv)
    return np.where(idx[:, None] >= idx[None, :], lv, -1).astype(np.float32)


def _causal_bias(t):
    idx = np.arange(t)
    return np.where(idx[:, None] >= idx[None, :], 0.0, -np.inf).astype(np.float32)


def _pack_in_proj_kernel(depth, wt_ref, *refs):
    outs, carry = refs[:depth], refs[depth]
    j = pl.program_id(0)
    lr_block = (COL_LR + LANES) // PACK_COLS - 1
    real_rows = COL_LR + GLA_RANK - lr_block * PACK_COLS
    row = lax.broadcasted_iota(jnp.int32, (PACK_COLS, D_MODEL), 0)

    @pl.when(j == 0)
    def _init_carry():
        carry[...] = jnp.zeros(carry.shape, F32)

    for l in range(depth):
        x = wt_ref[l]
        shifted = jnp.concatenate([carry[l], x[:PACK_COLS - PACK_PAD]], axis=0)
        padded = jnp.where(row < real_rows, x, 0.0)
        y = jnp.where(j > lr_block, shifted, jnp.where(j == lr_block, padded, x))
        outs[l][...] = y.T.astype(BF16)
        carry[l] = x[PACK_COLS - PACK_PAD:]


def _pack_in_proj(w_in):
    depth, _, d_in = w_in.shape
    assert d_in == sum(SPLIT_SIZES) and COL_END == d_in + PACK_PAD
    assert COL_END % PACK_COLS == 0 and (COL_LR + LANES) % PACK_COLS == 0
    last_block = (d_in - 1) // PACK_COLS
    return pl.pallas_call(
        functools.partial(_pack_in_proj_kernel, depth),
        grid=(COL_END // PACK_COLS,),
        in_specs=[pl.BlockSpec((depth, PACK_COLS, D_MODEL), lambda j: (0, jnp.minimum(j, last_block), 0))],
        out_specs=[pl.BlockSpec((D_MODEL, PACK_COLS), lambda j: (0, j))] * depth,
        out_shape=[jax.ShapeDtypeStruct((D_MODEL, COL_END), BF16)] * depth,
        scratch_shapes=[pltpu.VMEM((depth, PACK_PAD, D_MODEL), F32)],
        compiler_params=pltpu.CompilerParams(dimension_semantics=("arbitrary",),
                                             vmem_limit_bytes=VMEM_LIMIT_BYTES),
        name="pack_in_proj",
    )(jnp.swapaxes(w_in, 1, 2))


def _pack_out_proj_kernel(depth, w_br_ref, w_o_ref, *out_refs):
    for l in range(depth):
        out_refs[l][...] = w_br_ref[l].astype(BF16)
        out_refs[depth + l][...] = w_o_ref[l].astype(BF16)


def _pack_out_proj(w_branch, w_out):
    depth = w_branch.shape[0]
    n_steps = D_MODEL // PACK_ROWS
    br_rows = N_BRANCH * W_BRANCH // n_steps
    outs = pl.pallas_call(
        functools.partial(_pack_out_proj_kernel, depth),
        grid=(n_steps,),
        in_specs=[pl.BlockSpec((depth, br_rows, D_MODEL), lambda r: (0, r, 0)),
                  pl.BlockSpec((depth, PACK_ROWS, D_MODEL), lambda r: (0, r, 0))],
        out_specs=([pl.BlockSpec((br_rows, D_MODEL), lambda r: (r, 0))] * depth
                   + [pl.BlockSpec((PACK_ROWS, D_MODEL), lambda r: (r, 0))] * depth),
        out_shape=([jax.ShapeDtypeStruct((N_BRANCH * W_BRANCH, D_MODEL), BF16)] * depth
                   + [jax.ShapeDtypeStruct((D_MODEL, D_MODEL), BF16)] * depth),
        compiler_params=pltpu.CompilerParams(dimension_semantics=("arbitrary",)),
        name="pack_out_proj",
    )(w_branch.reshape(depth, N_BRANCH * W_BRANCH, D_MODEL), w_out)
    return [w.reshape(N_BRANCH, W_BRANCH, D_MODEL) for w in outs[:depth]], outs[depth:]


def _pack_layer(l, lru_conv_w, lru_conv_b, lru_wa, lru_ba, lru_wx, lru_bx, lru_lambda,
                m_conv_w, m_conv_b, m_wq, m_wk, m_wv, m_wi, m_bi, m_wf, m_bf, m_norm_w, m_skip,
                h_norm_w, g_w_lr2, g_b_lr2, g_norm_w):
    lru_w = jnp.concatenate([_block_diag_tiles(lru_wa[l]), _block_diag_tiles(lru_wx[l])], axis=2)
    m_qk = jnp.concatenate([_block_diag_tiles(m_wq[l]), _block_diag_tiles(m_wk[l])], axis=2)
    m_v = _block_diag_tiles(m_wv[l])
    w_gate = jnp.zeros((3 * W_BRANCH, 2 * LANES), F32)
    w_gate = w_gate.at[:, :N_HEADS].set(m_wi[l]).at[:, LANES:LANES + N_HEADS].set(m_wf[l])
    b_gate = jnp.zeros((1, 2 * LANES), F32)
    b_gate = b_gate.at[0, :N_HEADS].set(m_bi[l]).at[0, LANES:LANES + N_HEADS].set(m_bf[l])
    w_lr2 = jnp.pad(g_w_lr2[l], ((0, LANES - GLA_RANK), (0, 0)))
    rows = [lru_conv_w[l][j] for j in range(CONV_WIDTH)]
    rows += [lru_conv_b[l], lru_ba[l], lru_bx[l], lru_lambda[l]]
    rows += [m_conv_w[l][j] for j in range(CONV_WIDTH)]
    rows += [m_conv_b[l], m_norm_w[l], m_skip[l], jnp.tile(h_norm_w[l], N_HEADS), jnp.tile(g_norm_w[l], N_HEADS),
             jnp.pad(g_b_lr2[l], (0, W_BRANCH - GLA_DK))]
    vecs = jnp.stack(rows).astype(F32)
    vecs = jnp.pad(vecs, ((0, N_VEC_ROWS - vecs.shape[0]), (0, 0)))
    return lru_w, m_qk, m_v, w_gate.astype(BF16), b_gate, w_lr2.astype(BF16), vecs


def _layer_call(layer, final_norm, batch, seq):
    block_rows = TILE * TILES_PER_STEP
    n_tiles = seq // block_rows
    tile_spec = pl.BlockSpec((block_rows, D_MODEL), lambda b, j: (b * n_tiles + j, 0))
    resident = pl.BlockSpec(memory_space=pltpu.VMEM)
    state = functools.partial(pltpu.VMEM, dtype=F32)
    return pl.pallas_call(
        functools.partial(_layer_kernel, layer, final_norm),
        grid=(batch, n_tiles),
        in_specs=[tile_spec] + [resident] * 15,
        out_specs=tile_spec,
        out_shape=jax.ShapeDtypeStruct((batch * seq, D_MODEL), F32),
        scratch_shapes=[
            state((SUBLANES, W_BRANCH)),
            state((SUBLANES, W_BRANCH)),
            state((SUBLANES, W_BRANCH)),
            state((N_HEADS, HEAD_DIM, HEAD_DIM)),
            state((SUBLANES, W_BRANCH)),
            state((SUBLANES, LANES)),
            state((N_HEADS, HEAD_DIM, HEAD_DIM)),
            state((N_HEADS, HEAD_DIM, LANES)),
        ],
        compiler_params=pltpu.CompilerParams(
            dimension_semantics=("arbitrary", "arbitrary"),
            vmem_limit_bytes=VMEM_LIMIT_BYTES,
        ),
        name=f"hybrid_layer_{layer}",
    )


def kernel(x, norm_g, w_in, lru_conv_w, lru_conv_b, lru_wa, lru_ba, lru_wx, lru_bx, lru_lambda, m_conv_w, m_conv_b, m_wq, m_wk, m_wv, m_wi, m_bi, m_wf, m_bf, m_norm_w, m_skip, h_lb_logits, h_norm_w, g_w_lr2, g_b_lr2, g_norm_w, w_branch, w_out, final_g):
    batch, seq, d = x.shape
    depth = w_in.shape[0]
    assert d == D_MODEL and seq % (TILE * TILES_PER_STEP) == 0
    lvl = jnp.asarray(_level_table(TILE // 2), BF16)
    causal_bias = jnp.asarray(_causal_bias(TILE // 2))
    lb_logits = h_lb_logits.astype(F32)
    w_all = _pack_in_proj(w_in)
    w_br, w_o = _pack_out_proj(w_branch, w_out)
    y = x.reshape(batch * seq, d).astype(F32)
    for l in range(depth):
        packed = _pack_layer(l, lru_conv_w, lru_conv_b, lru_wa, lru_ba, lru_wx, lru_bx, lru_lambda,
                             m_conv_w, m_conv_b, m_wq, m_wk, m_wv, m_wi, m_bi, m_wf, m_bf, m_norm_w, m_skip,
                             h_norm_w, g_w_lr2, g_b_lr2, g_norm_w)
        (lru_w, m_qk, m_v, w_gate, b_gate, w_lr2, vecs) = packed
        y = _layer_call(l, l == depth - 1, batch, seq)(
            y, norm_g[l].reshape(1, d), final_g.reshape(1, d), w_all[l], w_br[l], w_o[l], lru_w, m_qk, m_v, w_gate,
            b_gate, w_lr2, vecs, lb_logits, lvl, causal_bias)
    return y.reshape(batch, seq, d)
```
